```python
import jax, jax.numpy as jnp
from jax import lax
import numpy as np

D_MODEL = 2048
BATCH = 1
SEQ = 8192
DEPTH = 1
DEC_BATCH = 128
DEC_SEQ = 1
PAST_LEN = 2048
PAGE_SIZE = 128

RET_HEADS = 8
RET_DK = D_MODEL // 16
RET_DV = 2 * RET_DK
RET_CHUNK = 128
RET_ROPE_BASE = 10000.0
MOBA_HEADS = 8
MOBA_DH = D_MODEL // 16
MOBA_BLOCK = 256
MOBA_TOPK = 3
MOBA_QBLOCK = 64
NORM_EPS = 1e-6
NEG_INF = -1e30

kernel_name = "retention_moba_gated_parallel_decoder_step"


def rms_norm(x, gain):
    xf = x.astype(jnp.float32)
    y = xf * lax.rsqrt(jnp.mean(xf * xf, axis=-1, keepdims=True) + NORM_EPS)
    return (y * gain.astype(jnp.float32)).astype(x.dtype)


def split_projection(z):
    widths = [RET_HEADS * RET_DK] * 2 + [RET_HEADS * RET_DV] * 2 + [MOBA_HEADS * MOBA_DH] * 4 + [2 * D_MODEL]
    cuts = [int(c) for c in np.cumsum(widths)[:-1]]
    return jnp.split(z, cuts, axis=-1)


def rotary(x, pos):
    half = x.shape[-1] // 2
    inv = 1.0 / (RET_ROPE_BASE ** jnp.linspace(0.0, 1.0, half, dtype=jnp.float32))
    ang = pos.astype(jnp.float32)[:, None] * inv[None, :]
    cos = jnp.cos(ang)[None, :, None, :]
    sin = jnp.sin(ang)[None, :, None, :]
    xf = x.astype(jnp.float32)
    x1, x2 = xf[..., :half], xf[..., half:]
    return jnp.concatenate([x1 * cos - x2 * sin, x1 * sin + x2 * cos], axis=-1)


def retention_chunkwise(q, k, v, s0, chunk):
    b, t, h, dk = q.shape
    dv = v.shape[-1]
    nc = t // chunk
    log_g = jnp.log(1.0 - 2.0 ** (-5.0 - jnp.arange(h, dtype=jnp.float32)))
    i = jnp.arange(chunk, dtype=jnp.float32)
    diff = i[:, None] - i[None, :]
    decay_in = jnp.where(diff[None] >= 0, jnp.exp(log_g[:, None, None] * jnp.maximum(diff, 0.0)[None]), 0.0)
    dec_q = jnp.exp(log_g[None, :] * (i[:, None] + 1.0))
    dec_k = jnp.exp(log_g[None, :] * (chunk - 1.0 - i[:, None]))
    dec_c = jnp.exp(log_g * chunk)

    def to_chunks(a):
        return a.reshape(b, nc, chunk, h, a.shape[-1]).transpose(1, 0, 2, 3, 4)

    def step(s, inp):
        qc, kc, vc = inp
        att = jnp.einsum('bihd,bjhd->bhij', qc, kc) * decay_in[None]
        inner = jnp.einsum('bhij,bjhe->bihe', att, vc)
        cross = jnp.einsum('bihd,bhde->bihe', qc, s) * dec_q[None, :, :, None]
        s = dec_c[None, :, None, None] * s + jnp.einsum('bjhd,bjhe->bhde', kc * dec_k[None, :, :, None], vc)
        return s, inner + cross

    s, ys = lax.scan(step, s0, (to_chunks(q), to_chunks(k), to_chunks(v)))
    return ys.transpose(1, 0, 2, 3, 4).reshape(b, t, h, dv), s


def retention_branch(q, k, v, pos, s0):
    b, t, _ = q.shape
    q = rotary(q.reshape(b, t, RET_HEADS, RET_DK), pos)
    k = rotary(k.reshape(b, t, RET_HEADS, RET_DK), pos) * (RET_DK ** -0.5)
    v = v.reshape(b, t, RET_HEADS, RET_DV).astype(jnp.float32)
    chunk = RET_CHUNK if t % RET_CHUNK == 0 else t
    y, s = retention_chunkwise(q, k, v, s0.astype(jnp.float32), chunk)
    y = y * lax.rsqrt(jnp.mean(y * y, axis=-1, keepdims=True) + NORM_EPS)
    return y.reshape(b, t, RET_HEADS * RET_DV), s


def to_blocks(parts):
    b, _, h, dh = parts[0].shape
    length = sum(p.shape[1] for p in parts)
    nb = -(-length // MOBA_BLOCK)
    pad = nb * MOBA_BLOCK - length
    if pad:
        parts = parts + [jnp.zeros((b, pad, h, dh), parts[0].dtype)]
    full = parts[0] if len(parts) == 1 else jnp.concatenate(parts, axis=1)
    return full.reshape(b, nb, MOBA_BLOCK, h, dh)


def moba_attention(q, kb, vb, q_offset):
    b, t, h, dh = q.shape
    nb = kb.shape[1]
    kmean = jnp.mean(kb, axis=2, dtype=jnp.float32)
    qbs = min(MOBA_QBLOCK, t)
    nq = -(-t // qbs)
    tp = nq * qbs
    if tp != t:
        q = jnp.pad(q, ((0, 0), (0, tp - t), (0, 0), (0, 0)))
    qs = q.reshape(b, nq, qbs, h, dh).transpose(1, 0, 2, 3, 4)
    pos_all = (q_offset + jnp.arange(tp, dtype=jnp.int32)).reshape(nq, qbs)
    n_sel = min(MOBA_TOPK, nb)
    scale = dh ** -0.5
    bi = jnp.arange(b)[:, None, None, None]
    hi = jnp.arange(h)[None, None, :, None]
    blk = jnp.arange(nb, dtype=jnp.int32)
    slot = jnp.arange(n_sel, dtype=jnp.int32)
    off = jnp.arange(MOBA_BLOCK, dtype=jnp.int32)

    def step(args):
        qc, pos = args
        own = pos // MOBA_BLOCK
        s = jnp.einsum('bthd,bnhd->bthn', qc.astype(jnp.float32), kmean)
        past = blk[None, :] < own[:, None]
        s = jnp.where(past[None, :, None, :], s, NEG_INF)
        _, top = lax.top_k(s, n_sel)
        own_idx = jnp.broadcast_to(jnp.minimum(own, nb - 1)[None, :, None, None], (b, qbs, h, 1)).astype(top.dtype)
        idx = jnp.concatenate([top, own_idx], axis=-1)
        kg = kb[bi, idx, :, hi]
        vg = vb[bi, idx, :, hi]
        logits = jnp.einsum('bthd,bthnsd->bthns', qc, kg, preferred_element_type=jnp.float32) * scale
        sel_ok = jnp.broadcast_to((slot[None, :] < own[:, None])[:, :, None], (qbs, n_sel, MOBA_BLOCK))
        own_ok = (own[:, None] * MOBA_BLOCK + off[None, :]) <= pos[:, None]
        ok = jnp.concatenate([sel_ok, own_ok[:, None, :]], axis=1)
        logits = jnp.where(ok[None, :, None], logits, NEG_INF)
        p = jax.nn.softmax(logits.reshape(b, qbs, h, -1), axis=-1).reshape(logits.shape)
        return jnp.einsum('bthns,bthnsd->bthd', p.astype(vg.dtype), vg,
                          preferred_element_type=jnp.float32).astype(q.dtype)

    out = lax.map(step, (qs, pos_all))
    return out.transpose(1, 0, 2, 3, 4).reshape(b, tp, h, dh)[:, :t]


def layer(x, q_offset, s0, k_past, v_past, norm_pre, w_in, w_a_out, w_b_out, b_merge, w_out, norm_post):
    b, t, _ = x.shape
    dt = x.dtype
    hn = rms_norm(x, norm_pre)
    z = jnp.einsum('btd,de->bte', hn, w_in)
    q_a, k_a, v_a, g_a, q_b, k_b, v_b, g_b, z_m = split_projection(z)
    pos = q_offset + jnp.arange(t, dtype=jnp.int32)
    y_a, s_new = retention_branch(q_a, k_a, v_a, pos, s0)
    q_b = q_b.reshape(b, t, MOBA_HEADS, MOBA_DH)
    k_b = k_b.reshape(b, t, MOBA_HEADS, MOBA_DH)
    v_b = v_b.reshape(b, t, MOBA_HEADS, MOBA_DH)
    kb = to_blocks([k_b] if k_past is None else [k_past, k_b])
    vb = to_blocks([v_b] if v_past is None else [v_past, v_b])
    y_b = moba_attention(q_b, kb, vb, q_offset).reshape(b, t, MOBA_HEADS * MOBA_DH)
    u_a = jnp.einsum('btc,cd->btd', (jax.nn.silu(g_a.astype(jnp.float32)) * y_a).astype(dt), w_a_out)
    u_b = jnp.einsum('btc,cd->btd', jax.nn.silu(g_b) * y_b, w_b_out)
    gates = jax.nn.sigmoid(z_m.reshape(b, t, 2, D_MODEL).astype(jnp.float32) + b_merge.astype(jnp.float32))
    m = (gates[..., 0, :] * u_a.astype(jnp.float32) + gates[..., 1, :] * u_b.astype(jnp.float32)).astype(dt)
    o = jnp.einsum('btd,de->bte', m, w_out)
    return x + rms_norm(o, norm_post), s_new.astype(dt), k_b, v_b


def setup_inputs(seed: int = 0) -> dict:
    key = jax.random.key(seed)
    ks = jax.random.split(key, 16)
    n_pages = PAST_LEN // PAGE_SIZE
    n_used = DEC_BATCH * n_pages
    n_phys = n_used + n_used // 4
    proj_w = 2 * RET_HEADS * RET_DK + 2 * RET_HEADS * RET_DV + 4 * MOBA_HEADS * MOBA_DH + 2 * D_MODEL
    f32 = jnp.float32
    x_prompt = jax.random.normal(ks[0], (BATCH, SEQ, D_MODEL), f32)
    x_sample = jax.random.normal(ks[1], (DEC_BATCH, DEC_SEQ, D_MODEL), f32)
    state_ret = jax.random.normal(ks[2], (DEPTH, DEC_BATCH, RET_HEADS, RET_DK, RET_DV), f32)
    cache_k = jax.random.normal(ks[3], (DEPTH, n_phys, PAGE_SIZE, MOBA_HEADS, MOBA_DH), f32)
    cache_v = jax.random.normal(ks[4], (DEPTH, n_phys, PAGE_SIZE, MOBA_HEADS, MOBA_DH), f32)
    page_table = jax.random.permutation(ks[5], n_phys)[:n_used].reshape(DEC_BATCH, n_pages).astype(jnp.int32)
    norm_pre = 1.0 + 0.02 * jax.random.normal(ks[6], (DEPTH, D_MODEL), f32)
    w_in = jax.random.normal(ks[7], (DEPTH, D_MODEL, proj_w), f32) * D_MODEL ** -0.5
    w_a_out = jax.random.normal(ks[8], (DEPTH, RET_HEADS * RET_DV, D_MODEL), f32) * (RET_HEADS * RET_DV) ** -0.5
    w_b_out = jax.random.normal(ks[9], (DEPTH, MOBA_HEADS * MOBA_DH, D_MODEL), f32) * (MOBA_HEADS * MOBA_DH) ** -0.5
    b_merge = 0.02 * jax.random.normal(ks[10], (DEPTH, 2, D_MODEL), f32)
    w_out = jax.random.normal(ks[11], (DEPTH, D_MODEL, D_MODEL), f32) * D_MODEL ** -0.5
    norm_post = 1.0 + 0.02 * jax.random.normal(ks[12], (DEPTH, D_MODEL), f32)
    return {"x_prompt": x_prompt, "x_sample": x_sample, "state_ret": state_ret,
            "cache_k": cache_k, "cache_v": cache_v, "page_table": page_table,
            "norm_pre": norm_pre, "w_in": w_in, "w_a_out": w_a_out, "w_b_out": w_b_out,
            "b_merge": b_merge, "w_out": w_out, "norm_post": norm_post}


def reference(x_prompt, x_sample, state_ret, cache_k, cache_v, page_table, norm_pre, w_in,
              w_a_out, w_b_out, b_merge, w_out, norm_post):
    n_pages = PAST_LEN // PAGE_SIZE
    n_seq = page_table.shape[0]
    y_p, y_s = x_prompt, x_sample
    sp_l, ss_l, kp_l, vp_l, ksm_l, vsm_l = [], [], [], [], [], []
    for l in range(DEPTH):
        w = (norm_pre[l], w_in[l], w_a_out[l], w_b_out[l], b_merge[l], w_out[l], norm_post[l])
        s_zero = jnp.zeros((y_p.shape[0], RET_HEADS, RET_DK, RET_DV), jnp.float32)
        y_p, s_p, k_p, v_p = layer(y_p, 0, s_zero, None, None, *w)
        k_past = cache_k[l, page_table].reshape(n_seq, n_pages * PAGE_SIZE, MOBA_HEADS, MOBA_DH).astype(y_s.dtype)
        v_past = cache_v[l, page_table].reshape(n_seq, n_pages * PAGE_SIZE, MOBA_HEADS, MOBA_DH).astype(y_s.dtype)
        y_s, s_s, k_s, v_s = layer(y_s, PAST_LEN, state_ret[l], k_past, v_past, *w)
        sp_l.append(s_p); ss_l.append(s_s)
        kp_l.append(k_p); vp_l.append(v_p)
        ksm_l.append(k_s); vsm_l.append(v_s)
    return (y_p, y_s, jnp.stack(sp_l), jnp.stack(ss_l), jnp.stack(kp_l), jnp.stack(vp_l), jnp.stack(ksm_l), jnp.stack(vsm_l))
```

```python
import functools

import jax
import jax.numpy as jnp
from jax import lax
from jax.experimental import pallas as pl
from jax.experimental.pallas import tpu as pltpu

F32 = jnp.float32
BF16 = jnp.bfloat16

D_MODEL = 2048
SEQ = 8192
DEC_BATCH = 128
PAST_LEN = 2048
PAGE_SIZE = 128
N_PAGES = PAST_LEN // PAGE_SIZE
HEADS = 8
RET_DK = 128
RET_DV = 256
RET_CHUNK = 128
RET_ROPE_BASE = 10000.0
MOBA_DH = 128
MOBA_BLOCK = 256
MOBA_TOPK = 3
NORM_EPS = 1e-6
NEG_INF = -1e30

COL_QA, COL_KA, COL_VA, COL_GA = 0, 1024, 2048, 4096
COL_QB, COL_KB, COL_VB, COL_GB, COL_ZM = 6144, 7168, 8192, 9216, 10240
PROJ_W = 14336

V7X_VMEM_LIMIT = 56 * 1024 * 1024

NT_DIMS = (((1,), (1,)), ((), ()))
TN_DIMS = (((0,), (0,)), ((), ()))


def _params(n_axes, vmem=V7X_VMEM_LIMIT):
    return pltpu.CompilerParams(dimension_semantics=("arbitrary",) * n_axes, vmem_limit_bytes=vmem)


def _rmsnorm_kernel(x_ref, g_ref, o_ref):
    x = x_ref[...]
    y = x * lax.rsqrt(jnp.mean(x * x, axis=-1, keepdims=True) + NORM_EPS)
    o_ref[...] = (y * g_ref[...]).astype(o_ref.dtype)


def _rmsnorm(x, gain, tile):
    rows = x.shape[0]
    return pl.pallas_call(
        _rmsnorm_kernel,
        grid=(rows // tile,),
        in_specs=[pl.BlockSpec((tile, D_MODEL), lambda i: (i, 0)),
                  pl.BlockSpec((1, D_MODEL), lambda i: (0, 0))],
        out_specs=pl.BlockSpec((tile, D_MODEL), lambda i: (i, 0)),
        out_shape=jax.ShapeDtypeStruct((rows, D_MODEL), BF16),
        compiler_params=_params(1),
        name="rmsnorm_pre",
    )(x, gain)


IN_TM = 512
IN_TN = 1024
IN_MP = SEQ // IN_TM


def _in_proj_kernel(hp_ref, hs_ref, w_ref, zp_ref, zs_ref, wb_ref):
    m = pl.program_id(1)

    @pl.when(m == 0)
    def _():
        wb_ref[...] = w_ref[...].astype(BF16)

    @pl.when(m < IN_MP)
    def _():
        zp_ref[...] = jnp.dot(hp_ref[...], wb_ref[...], preferred_element_type=F32)

    @pl.when(m == IN_MP)
    def _():
        zs_ref[...] = jnp.dot(hs_ref[...], wb_ref[...], preferred_element_type=F32)


def _in_proj(hn_p, hn_s, w_in):
    last = IN_MP - 1
    return pl.pallas_call(
        _in_proj_kernel,
        grid=(PROJ_W // IN_TN, IN_MP + 1),
        in_specs=[pl.BlockSpec((IN_TM, D_MODEL), lambda n, m: (jnp.minimum(m, last), 0)),
                  pl.BlockSpec((DEC_BATCH, D_MODEL), lambda n, m: (0, 0)),
                  pl.BlockSpec((D_MODEL, IN_TN), lambda n, m: (0, n))],
        out_specs=[pl.BlockSpec((IN_TM, IN_TN), lambda n, m: (jnp.minimum(m, last), n)),
                   pl.BlockSpec((DEC_BATCH, IN_TN), lambda n, m: (0, n))],
        out_shape=[jax.ShapeDtypeStruct((SEQ, PROJ_W), F32),
                   jax.ShapeDtypeStruct((DEC_BATCH, PROJ_W), F32)],
        scratch_shapes=[pltpu.VMEM((D_MODEL, IN_TN), BF16)],
        compiler_params=_params(2),
        name="in_proj",
    )(hn_p, hn_s, w_in)


def _rotate(x, cos2, sin2):
    return x * cos2 + pltpu.roll(x, RET_DK // 2, 1) * sin2


def _group_norm_gate(y, g):
    yn = y * lax.rsqrt(jnp.mean(y * y, axis=-1, keepdims=True) + NORM_EPS)
    return (g * jax.nn.sigmoid(g)) * yn


def _ret_prompt_kernel(dec_ref, q_ref, k_ref, v_ref, g_ref, cos_ref, sin_ref, ga_ref, st_ref,
                       s_scr, din_scr, dq_scr, dk_scr):
    c = pl.program_id(0)
    C = RET_CHUNK

    @pl.when(c == 0)
    def _():
        ii = lax.broadcasted_iota(jnp.int32, (C, C), 0).astype(F32)
        jj = lax.broadcasted_iota(jnp.int32, (C, C), 1).astype(F32)
        diff = ii - jj
        for h in range(HEADS):
            lg = dec_ref[0, h]
            din_scr[h] = jnp.where(diff >= 0, jnp.exp(lg * jnp.maximum(diff, 0.0)), 0.0)
            dq_scr[h] = jnp.exp(lg * (ii + 1.0))
            dk_scr[h] = jnp.exp(lg * (C - 1.0 - ii))
        s_scr[...] = jnp.zeros_like(s_scr)

    cos2 = cos_ref[...]
    sin2 = sin_ref[...]
    for h in range(HEADS):
        qr = _rotate(q_ref[:, h * RET_DK:(h + 1) * RET_DK], cos2, sin2)
        kr = _rotate(k_ref[:, h * RET_DK:(h + 1) * RET_DK], cos2, sin2) * (RET_DK ** -0.5)
        vb = v_ref[:, h * RET_DV:(h + 1) * RET_DV].astype(BF16)
        att = lax.dot_general(qr.astype(BF16), kr.astype(BF16), NT_DIMS, preferred_element_type=F32)
        inner = jnp.dot((att * din_scr[h]).astype(BF16), vb, preferred_element_type=F32)
        s_old = s_scr[h]
        cross = jnp.dot((qr * dq_scr[h]).astype(BF16), s_old.astype(BF16), preferred_element_type=F32)
        kv = lax.dot_general((kr * dk_scr[h]).astype(BF16), vb, TN_DIMS, preferred_element_type=F32)
        s_scr[h] = dec_ref[1, h] * s_old + kv
        gated = _group_norm_gate(inner + cross, g_ref[:, h * RET_DV:(h + 1) * RET_DV])
        ga_ref[:, h * RET_DV:(h + 1) * RET_DV] = gated.astype(ga_ref.dtype)

    @pl.when(c == pl.num_programs(0) - 1)
    def _():
        st_ref[...] = s_scr[...]


def _ret_prompt(dec, z_p, cos2, sin2):
    C = RET_CHUNK
    grid_spec = pltpu.PrefetchScalarGridSpec(
        num_scalar_prefetch=1,
        grid=(SEQ // C,),
        in_specs=[pl.BlockSpec((C, HEADS * RET_DK), lambda c, d: (c, COL_QA // 1024)),
                  pl.BlockSpec((C, HEADS * RET_DK), lambda c, d: (c, COL_KA // 1024)),
                  pl.BlockSpec((C, HEADS * RET_DV), lambda c, d: (c, COL_VA // 2048)),
                  pl.BlockSpec((C, HEADS * RET_DV), lambda c, d: (c, COL_GA // 2048)),
                  pl.BlockSpec((C, RET_DK), lambda c, d: (c, 0)),
                  pl.BlockSpec((C, RET_DK), lambda c, d: (c, 0))],
        out_specs=[pl.BlockSpec((C, HEADS * RET_DV), lambda c, d: (c, 0)),
                   pl.BlockSpec((HEADS, RET_DK, RET_DV), lambda c, d: (0, 0, 0))],
        scratch_shapes=[pltpu.VMEM((HEADS, RET_DK, RET_DV), F32),
                        pltpu.VMEM((HEADS, C, C), F32),
                        pltpu.VMEM((HEADS, C, C), F32),
                        pltpu.VMEM((HEADS, C, C), F32)],
    )
    return pl.pallas_call(
        _ret_prompt_kernel,
        grid_spec=grid_spec,
        out_shape=[jax.ShapeDtypeStruct((SEQ, HEADS * RET_DV), BF16),
                   jax.ShapeDtypeStruct((HEADS, RET_DK, RET_DV), F32)],
        compiler_params=_params(1),
        name="retention_prompt",
    )(dec, z_p, z_p, z_p, z_p, cos2, sin2)


RS_TB = 8


def _ret_sample_kernel(dec_ref, q_ref, k_ref, v_ref, g_ref, cos_ref, sin_ref, s0_ref, ga_ref, s1_ref, y_scr):
    i = pl.program_id(0)
    rows = pl.ds(pl.multiple_of(i * RS_TB, RS_TB), RS_TB)
    cos2 = cos_ref[...]
    sin2 = sin_ref[...]
    eye = (lax.broadcasted_iota(jnp.int32, (RET_DK, RET_DK), 0)
           == lax.broadcasted_iota(jnp.int32, (RET_DK, RET_DK), 1))
    ones = jnp.ones((RET_DK, RET_DV), BF16)

    def lane_bcast_column(row):
        diag = jnp.where(eye, jnp.broadcast_to(row, (RET_DK, RET_DK)), 0.0).astype(BF16)
        return jnp.dot(diag, ones, preferred_element_type=F32)

    for h in range(HEADS):
        gamma = dec_ref[2, h]
        qr = _rotate(q_ref[rows, h * RET_DK:(h + 1) * RET_DK], cos2, sin2)
        kr = _rotate(k_ref[rows, h * RET_DK:(h + 1) * RET_DK], cos2, sin2) * (RET_DK ** -0.5)
        vh = v_ref[rows, h * RET_DV:(h + 1) * RET_DV]
        qk = jnp.sum(qr * kr, axis=-1, keepdims=True)
        inner = qk * vh
        for b in range(RS_TB):
            s_old = s0_ref[b, h]
            cross = jnp.sum(lane_bcast_column(qr[b:b + 1, :]) * s_old, axis=0, keepdims=True)
            s1_ref[b, h] = gamma * s_old + lane_bcast_column(kr[b:b + 1, :]) * vh[b:b + 1, :]
            y_scr[b:b + 1, h * RET_DV:(h + 1) * RET_DV] = inner[b:b + 1, :] + cross * gamma
    for h in range(HEADS):
        cols = slice(h * RET_DV, (h + 1) * RET_DV)
        ga_ref[:, cols] = _group_norm_gate(y_scr[:, cols], g_ref[rows, cols])


def _ret_sample(dec, z_s, cos2, sin2, state):
    full = lambda shape, col: pl.BlockSpec(shape, lambda i, d: (0, col))
    grid_spec = pltpu.PrefetchScalarGridSpec(
        num_scalar_prefetch=1,
        grid=(DEC_BATCH // RS_TB,),
        in_specs=[full((DEC_BATCH, HEADS * RET_DK), COL_QA // 1024),
                  full((DEC_BATCH, HEADS * RET_DK), COL_KA // 1024),
                  full((DEC_BATCH, HEADS * RET_DV), COL_VA // 2048),
                  full((DEC_BATCH, HEADS * RET_DV), COL_GA // 2048),
                  full((1, RET_DK), 0),
                  full((1, RET_DK), 0),
                  pl.BlockSpec((RS_TB, HEADS, RET_DK, RET_DV), lambda i, d: (i, 0, 0, 0))],
        out_specs=[pl.BlockSpec((RS_TB, HEADS * RET_DV), lambda i, d: (i, 0)),
                   pl.BlockSpec((RS_TB, HEADS, RET_DK, RET_DV), lambda i, d: (i, 0, 0, 0))],
        scratch_shapes=[pltpu.VMEM((RS_TB, HEADS * RET_DV), F32)],
    )
    return pl.pallas_call(
        _ret_sample_kernel,
        grid_spec=grid_spec,
        out_shape=[jax.ShapeDtypeStruct((DEC_BATCH, HEADS * RET_DV), F32),
                   jax.ShapeDtypeStruct((DEC_BATCH, HEADS, RET_DK, RET_DV), F32)],
        compiler_params=_params(1),
        name="retention_sample",
    )(dec, z_s, z_s, z_s, z_s, cos2, sin2, state)


N_BLK = SEQ // MOBA_BLOCK
BIAS_W = 128
AUG_W = MOBA_DH + BIAS_W


def _moba_prep_kernel(q_ref, k_ref, v_ref, qa_ref, ka_ref, vb_ref, km_scr):
    i = pl.program_id(1)

    @pl.when(i == 0)
    def _():
        km_scr[...] = jnp.zeros_like(km_scr)

    q = q_ref[...]
    k = k_ref[...]
    s_t = lax.dot_general(km_scr[...], q, NT_DIMS, precision=lax.Precision.HIGHEST,
                          preferred_element_type=F32)
    blk = lax.broadcasted_iota(jnp.int32, (BIAS_W, MOBA_BLOCK), 0)
    rank = jnp.zeros((BIAS_W, MOBA_BLOCK), F32)
    for jp in range(N_BLK):
        sj = s_t[jp:jp + 1, :]
        beats = (sj > s_t) | ((sj == s_t) & (blk > jp))
        rank = rank + jnp.where(beats, jnp.where(jp < i, 1.0, 0.0), 0.0)
    keep = ((blk < i) & (rank < MOBA_TOPK)) | (blk == i)
    bias_t = jnp.where(keep, 0.0, NEG_INF)
    qa_ref[:, :MOBA_DH] = (q * (MOBA_DH ** -0.5)).astype(BF16)
    qa_ref[:, MOBA_DH:] = bias_t.T.astype(BF16)
    lane = lax.broadcasted_iota(jnp.int32, (MOBA_BLOCK, BIAS_W), 1)
    ka_ref[:, :MOBA_DH] = k.astype(BF16)
    ka_ref[:, MOBA_DH:] = jnp.where(lane == i, 1.0, 0.0).astype(BF16)
    vb_ref[...] = v_ref[...].astype(BF16)
    km_scr[pl.ds(i, 1), :] = jnp.mean(k, axis=0, keepdims=True)


def _moba_prep(z_p):
    col = lambda base: (lambda h, i: (i, base // MOBA_DH + h))
    out_idx = lambda h, i: (h, i, 0)
    return pl.pallas_call(
        _moba_prep_kernel,
        grid=(HEADS, N_BLK),
        in_specs=[pl.BlockSpec((MOBA_BLOCK, MOBA_DH), col(COL_QB)),
                  pl.BlockSpec((MOBA_BLOCK, MOBA_DH), col(COL_KB)),
                  pl.BlockSpec((MOBA_BLOCK, MOBA_DH), col(COL_VB))],
        out_specs=[pl.BlockSpec((None, MOBA_BLOCK, AUG_W), out_idx),
                   pl.BlockSpec((None, MOBA_BLOCK, AUG_W), out_idx),
                   pl.BlockSpec((None, MOBA_BLOCK, MOBA_DH), out_idx)],
        out_shape=[jax.ShapeDtypeStruct((HEADS, SEQ, AUG_W), BF16),
                   jax.ShapeDtypeStruct((HEADS, SEQ, AUG_W), BF16),
                   jax.ShapeDtypeStruct((HEADS, SEQ, MOBA_DH), BF16)],
        scratch_shapes=[pltpu.VMEM((BIAS_W, MOBA_DH), F32)],
        compiler_params=_params(2),
        name="moba_prep",
    )(z_p, z_p, z_p)


def _moba_flash_kernel(q_ref, k_ref, v_ref, g_ref, o_ref):
    i = pl.program_id(1)
    q = q_ref[...]

    def block(j):
        r = pl.ds(pl.multiple_of(j * MOBA_BLOCK, MOBA_BLOCK), MOBA_BLOCK)
        s = lax.dot_general(q, k_ref[r, :], NT_DIMS, preferred_element_type=F32)
        return s, v_ref[r, :]

    s, v = block(i)
    row = lax.broadcasted_iota(jnp.int32, (MOBA_BLOCK, MOBA_BLOCK), 0)
    colm = lax.broadcasted_iota(jnp.int32, (MOBA_BLOCK, MOBA_BLOCK), 1)
    s = jnp.where(colm <= row, s, NEG_INF)
    m = jnp.max(s, axis=-1, keepdims=True)
    p = jnp.exp(s - m)
    l = jnp.sum(p, axis=-1, keepdims=True)
    acc = jnp.dot(p.astype(BF16), v, preferred_element_type=F32)

    def body(j, carry):
        m, l, acc = carry
        s, v = block(j)
        m_new = jnp.maximum(m, jnp.max(s, axis=-1, keepdims=True))
        alpha = jnp.exp(m - m_new)
        p = jnp.exp(s - m_new)
        l = alpha * l + jnp.sum(p, axis=-1, keepdims=True)
        acc = alpha * acc + jnp.dot(p.astype(BF16), v, preferred_element_type=F32)
        return m_new, l, acc

    m, l, acc = lax.fori_loop(0, i, body, (m, l, acc))
    g = g_ref[...]
    o_ref[...] = ((g * jax.nn.sigmoid(g)) * (acc / l)).astype(o_ref.dtype)


def _moba_flash(q_aug, k_aug, v_b, z_p):
    return pl.pallas_call(
        _moba_flash_kernel,
        grid=(HEADS, N_BLK),
        in_specs=[pl.BlockSpec((None, MOBA_BLOCK, AUG_W), lambda h, i: (h, i, 0)),
                  pl.BlockSpec((None, SEQ, AUG_W), lambda h, i: (h, 0, 0)),
                  pl.BlockSpec((None, SEQ, MOBA_DH), lambda h, i: (h, 0, 0)),
                  pl.BlockSpec((MOBA_BLOCK, MOBA_DH), lambda h, i: (i, COL_GB // MOBA_DH + h))],
        out_specs=pl.BlockSpec((MOBA_BLOCK, MOBA_DH), lambda h, i: (i, h)),
        out_shape=jax.ShapeDtypeStruct((SEQ, HEADS * MOBA_DH), BF16),
        compiler_params=_params(2),
        name="moba_flash",
    )(q_aug, k_aug, v_b, z_p)


PAST_BLOCKS = PAST_LEN // MOBA_BLOCK
HD = HEADS * MOBA_DH


PAGE_ROWS = PAGE_SIZE * HEADS
PAGES_PER_BLOCK = MOBA_BLOCK // PAGE_SIZE


def _per_head_allreduce(x, op):
    acc = x[:, 0:128]
    for c in range(1, x.shape[1] // 128):
        acc = op(acc, x[:, c * 128:(c + 1) * 128])
    shift = 64
    while shift >= HEADS:
        acc = op(acc, pltpu.roll(acc, shift, 1))
        shift //= 2
    return acc


def _moba_sample_kernel(pt_ref, x_ref, *refs):
    k_refs = refs[:N_PAGES]
    v_refs = refs[N_PAGES:2 * N_PAGES]
    o_ref = refs[2 * N_PAGES]
    q16 = (x_ref[0:HEADS, :] * (MOBA_DH ** -0.5)).astype(BF16)
    k_new = x_ref[HEADS:2 * HEADS, :]
    v_new = x_ref[2 * HEADS:3 * HEADS, :]
    g = x_ref[3 * HEADS:4 * HEADS, :]
    sub = lax.broadcasted_iota(jnp.int32, (HEADS, PAGE_ROWS), 0)
    lane = lax.broadcasted_iota(jnp.int32, (HEADS, PAGE_ROWS), 1)
    own = sub == (lane % HEADS)
    own_new = sub[:, 0:128] == lane[:, 0:128]

    def page_logits(k16, mask):
        r = lax.dot_general(q16, k16, NT_DIMS, preferred_element_type=F32)
        return jnp.sum(jnp.where(mask, r, 0.0), axis=0, keepdims=True)

    rows = [page_logits(k_refs[p][...].astype(BF16), own) for p in range(N_PAGES)]
    halves = [jnp.concatenate(rows[r::PAGES_PER_BLOCK], axis=0) for r in range(PAGES_PER_BLOCK)]
    zeros_tail = jnp.zeros((128 - HEADS, MOBA_DH), F32)
    k_self = jnp.concatenate([k_new, zeros_tail], axis=0).astype(BF16)
    s_self = _per_head_allreduce(page_logits(k_self, own_new), jnp.add)

    bsum = _per_head_allreduce(functools.reduce(jnp.add, halves), jnp.add)
    blk = lax.broadcasted_iota(jnp.int32, bsum.shape, 0)
    rank = jnp.zeros(bsum.shape, F32)
    for jp in range(PAST_BLOCKS):
        bj = bsum[jp:jp + 1, :]
        beats = (bj > bsum) | ((bj == bsum) & (blk > jp))
        rank = rank + jnp.where(beats, 1.0, 0.0)
    keep = jnp.where(rank < MOBA_TOPK, 1.0, 0.0)
    keep_full = jnp.concatenate([keep] * (PAGE_ROWS // 128), axis=1) > 0.5
    masked = [jnp.where(keep_full, hv, NEG_INF) for hv in halves]
    m_blk = _per_head_allreduce(functools.reduce(jnp.maximum, masked), jnp.maximum)
    m = jnp.maximum(jnp.max(m_blk, axis=0, keepdims=True), s_self)
    m_full = jnp.concatenate([m] * (PAGE_ROWS // 128), axis=1)
    probs = [jnp.exp(mv - m_full) for mv in masked]
    p_self = jnp.exp(s_self - m)
    l = jnp.sum(_per_head_allreduce(functools.reduce(jnp.add, probs), jnp.add), axis=0, keepdims=True) + p_self
    inv = 1.0 / l
    inv_full = jnp.concatenate([inv] * (PAGE_ROWS // 128), axis=1)
    probs = [pv * inv_full for pv in probs]

    v_self = jnp.concatenate([v_new, zeros_tail], axis=0).astype(BF16)
    w_self = jnp.where(own_new, jnp.broadcast_to(p_self * inv, (HEADS, 128)), 0.0).astype(BF16)
    acc = jnp.dot(w_self, v_self, preferred_element_type=F32)
    for p in range(N_PAGES):
        prow = probs[p % PAGES_PER_BLOCK][p // PAGES_PER_BLOCK:p // PAGES_PER_BLOCK + 1, :]
        w = jnp.where(own, jnp.broadcast_to(prow, (HEADS, PAGE_ROWS)), 0.0).astype(BF16)
        acc = acc + jnp.dot(w, v_refs[p][...].astype(BF16), preferred_element_type=F32)
    o_ref[...] = (g * jax.nn.sigmoid(g)) * acc


def _moba_sample(page_table, x_s, cache_k, cache_v):
    page = lambda p: pl.BlockSpec((None, PAGE_ROWS, MOBA_DH), lambda b, pt: (pt[b, p], 0, 0))
    grid_spec = pltpu.PrefetchScalarGridSpec(
        num_scalar_prefetch=1,
        grid=(DEC_BATCH,),
        in_specs=([pl.BlockSpec((None, 4 * HEADS, MOBA_DH), lambda b, pt: (b, 0, 0))]
                  + [page(p) for p in range(N_PAGES)] + [page(p) for p in range(N_PAGES)]),
        out_specs=pl.BlockSpec((None, HEADS, MOBA_DH), lambda b, pt: (b, 0, 0)),
    )
    return pl.pallas_call(
        _moba_sample_kernel,
        grid_spec=grid_spec,
        out_shape=jax.ShapeDtypeStruct((DEC_BATCH, HEADS, MOBA_DH), F32),
        compiler_params=_params(1),
        name="moba_sample",
    )(page_table, x_s, *([cache_k] * N_PAGES), *([cache_v] * N_PAGES))


def _out_proj_kernel(ga_ref, gb_ref, zma_ref, zmb_ref, x_ref, bm_ref, wa_ref, wb_ref, wo_ref, gp_ref, y_ref):
    u_a = jnp.dot(ga_ref[...], wa_ref[...], preferred_element_type=F32)
    u_b = jnp.dot(gb_ref[...], wb_ref[...], preferred_element_type=F32)
    gate_a = jax.nn.sigmoid(zma_ref[...] + bm_ref[0:1, :])
    gate_b = jax.nn.sigmoid(zmb_ref[...] + bm_ref[1:2, :])
    merged = (gate_a * u_a + gate_b * u_b).astype(BF16)
    o = jnp.dot(merged, wo_ref[...], preferred_element_type=F32)
    on = o * lax.rsqrt(jnp.mean(o * o, axis=-1, keepdims=True) + NORM_EPS)
    y_ref[...] = x_ref[...] + on * gp_ref[...]


def _out_proj(ga, gb, z, x, b_merge, w_a, w_b, w_o, g_post, tile):
    rows = x.shape[0]
    resident = lambda shape: pl.BlockSpec(shape, lambda i: (0, 0), pipeline_mode=pl.Buffered(1))
    return pl.pallas_call(
        _out_proj_kernel,
        grid=(rows // tile,),
        in_specs=[pl.BlockSpec((tile, HEADS * RET_DV), lambda i: (i, 0)),
                  pl.BlockSpec((tile, HD), lambda i: (i, 0)),
                  pl.BlockSpec((tile, D_MODEL), lambda i: (i, COL_ZM // D_MODEL)),
                  pl.BlockSpec((tile, D_MODEL), lambda i: (i, COL_ZM // D_MODEL + 1)),
                  pl.BlockSpec((tile, D_MODEL), lambda i: (i, 0)),
                  resident((2, D_MODEL)),
                  resident((HEADS * RET_DV, D_MODEL)),
                  resident((HD, D_MODEL)),
                  resident((D_MODEL, D_MODEL)),
                  resident((1, D_MODEL))],
        out_specs=pl.BlockSpec((tile, D_MODEL), lambda i: (i, 0)),
        out_shape=jax.ShapeDtypeStruct((rows, D_MODEL), F32),
        compiler_params=_params(1),
        name="out_proj",
    )(ga, gb, z, z, x, b_merge, w_a, w_b, w_o, g_post)


def _rope_tables(pos):
    half = RET_DK // 2
    inv = 1.0 / (RET_ROPE_BASE ** jnp.linspace(0.0, 1.0, half, dtype=F32))
    ang = pos.astype(F32)[:, None] * inv[None, :]
    cos, sin = jnp.cos(ang), jnp.sin(ang)
    return jnp.concatenate([cos, cos], axis=-1), jnp.concatenate([-sin, sin], axis=-1)


def kernel(x_prompt, x_sample, state_ret, cache_k, cache_v, page_table, norm_pre, w_in,
           w_a_out, w_b_out, b_merge, w_out, norm_post):
    xp = x_prompt.reshape(SEQ, D_MODEL)
    xs = x_sample.reshape(DEC_BATCH, D_MODEL)
    g_pre = norm_pre.reshape(1, D_MODEL)
    g_post = norm_post.reshape(1, D_MODEL)

    log_g = jnp.log(1.0 - 2.0 ** (-5.0 - jnp.arange(HEADS, dtype=F32)))
    dec = jnp.stack([log_g, jnp.exp(log_g * RET_CHUNK), jnp.exp(log_g * 1.0)])
    cos_p, sin_p = _rope_tables(jnp.arange(SEQ, dtype=jnp.int32))
    cos_s, sin_s = _rope_tables(PAST_LEN + jnp.arange(1, dtype=jnp.int32))

    hn_p = _rmsnorm(xp, g_pre, 1024)
    hn_s = _rmsnorm(xs, g_pre, DEC_BATCH)
    z_p, z_s = _in_proj(hn_p, hn_s, w_in.reshape(D_MODEL, PROJ_W))

    ga_p, st_p = _ret_prompt(dec, z_p, cos_p, sin_p)
    ga_s, st_s = _ret_sample(dec, z_s, cos_s, sin_s,
                             state_ret.reshape(DEC_BATCH, HEADS, RET_DK, RET_DV))

    q_aug, k_aug, v_b = _moba_prep(z_p)
    gb_p = _moba_flash(q_aug, k_aug, v_b, z_p)
    n_phys = cache_k.shape[1]
    gb_s = _moba_sample(page_table, z_s[:, COL_QB:COL_ZM].reshape(DEC_BATCH, 4 * HEADS, MOBA_DH),
                        cache_k.reshape(n_phys, PAGE_ROWS, MOBA_DH), cache_v.reshape(n_phys, PAGE_ROWS, MOBA_DH))

    w_a = w_a_out.reshape(HEADS * RET_DV, D_MODEL).astype(BF16)
    w_b = w_b_out.reshape(HD, D_MODEL).astype(BF16)
    w_o = w_out.reshape(D_MODEL, D_MODEL).astype(BF16)
    bm = b_merge.reshape(2, D_MODEL)
    y_p = _out_proj(ga_p, gb_p, z_p, xp, bm, w_a, w_b, w_o, g_post, 256)
    y_s = _out_proj(ga_s.astype(BF16), gb_s.reshape(DEC_BATCH, HD).astype(BF16), z_s, xs, bm,
                    w_a, w_b, w_o, g_post, DEC_BATCH)

    return (y_p.reshape(1, SEQ, D_MODEL),
            y_s.reshape(DEC_BATCH, 1, D_MODEL),
            st_p.reshape(1, 1, HEADS, RET_DK, RET_DV),
            st_s.reshape(1, DEC_BATCH, HEADS, RET_DK, RET_DV),
            z_p[:, COL_KB:COL_VB].reshape(1, 1, SEQ, HEADS, MOBA_DH),
            z_p[:, COL_VB:COL_GB].reshape(1, 1, SEQ, HEADS, MOBA_DH),
            z_s[:, COL_KB:COL_VB].reshape(1, DEC_BATCH, 1, HEADS, MOBA_DH),
            z_s[:, COL_VB:COL_GB].reshape(1, DEC_BATCH, 1, HEADS, MOBA_DH))
```

```python
import functools

import jax
import jax.numpy as jnp
from jax import lax
from jax.experimental import pallas as pl
from jax.experimental.pallas import tpu as pltpu

F32 = jnp.float32
BF16 = jnp.bfloat16

D_MODEL = 2048
SEQ = 8192
DEC_BATCH = 128
PAST_LEN = 2048
PAGE_SIZE = 128
N_PAGES = PAST_LEN // PAGE_SIZE
HEADS = 8
RET_DK = 128
RET_DV = 256
RET_CHUNK = 128
RET_ROPE_BASE = 10000.0
MOBA_DH = 128
HD = HEADS * MOBA_DH
MOBA_BLOCK = 256
MOBA_TOPK = 3
NORM_EPS = 1e-6
NEG_INF = -1e30

COL_QA, COL_KA, COL_VA, COL_GA = 0, 1024, 2048, 4096
COL_QB, COL_KB, COL_VB, COL_GB, COL_ZM = 6144, 7168, 8192, 9216, 10240
PROJ_W = 14336

V7X_VMEM_LIMIT = 56 * 1024 * 1024

NT_DIMS = (((1,), (1,)), ((), ()))
TN_DIMS = (((0,), (0,)), ((), ()))


def _params(n_axes, vmem=V7X_VMEM_LIMIT):
    return pltpu.CompilerParams(dimension_semantics=("arbitrary",) * n_axes, vmem_limit_bytes=vmem)


def _rmsnorm_kernel(x_ref, g_ref, o_ref):
    x = x_ref[...]
    y = x * lax.rsqrt(jnp.mean(x * x, axis=-1, keepdims=True) + NORM_EPS)
    o_ref[...] = (y * g_ref[...]).astype(o_ref.dtype)


def _rmsnorm(x, gain, tile):
    rows = x.shape[0]
    return pl.pallas_call(
        _rmsnorm_kernel,
        grid=(rows // tile,),
        in_specs=[pl.BlockSpec((tile, D_MODEL), lambda i: (i, 0)),
                  pl.BlockSpec((1, D_MODEL), lambda i: (0, 0))],
        out_specs=pl.BlockSpec((tile, D_MODEL), lambda i: (i, 0)),
        out_shape=jax.ShapeDtypeStruct((rows, D_MODEL), BF16),
        compiler_params=_params(1),
        name="rmsnorm_pre",
    )(x, gain)


IN_TM = 1024
IN_TN = 1024
IN_MP = SEQ // IN_TM


def _in_proj_kernel(hp_ref, hs_ref, w_ref, zp_ref, zs_ref, wb_ref):
    m = pl.program_id(1)

    @pl.when(m == 0)
    def _():
        wb_ref[...] = w_ref[...].astype(BF16)

    @pl.when(m < IN_MP)
    def _():
        zp_ref[...] = jnp.dot(hp_ref[...], wb_ref[...], preferred_element_type=F32)

    @pl.when(m == IN_MP)
    def _():
        zs_ref[...] = jnp.dot(hs_ref[...], wb_ref[...], preferred_element_type=F32)


def _in_proj(hn_p, hn_s, w_in):
    last = IN_MP - 1
    return pl.pallas_call(
        _in_proj_kernel,
        grid=(PROJ_W // IN_TN, IN_MP + 1),
        in_specs=[pl.BlockSpec((IN_TM, D_MODEL), lambda n, m: (jnp.minimum(m, last), 0)),
                  pl.BlockSpec((DEC_BATCH, D_MODEL), lambda n, m: (0, 0)),
                  pl.BlockSpec((D_MODEL, IN_TN), lambda n, m: (0, n))],
        out_specs=[pl.BlockSpec((IN_TM, IN_TN), lambda n, m: (jnp.minimum(m, last), n)),
                   pl.BlockSpec((DEC_BATCH, IN_TN), lambda n, m: (0, n))],
        out_shape=[jax.ShapeDtypeStruct((SEQ, PROJ_W), F32),
                   jax.ShapeDtypeStruct((DEC_BATCH, PROJ_W), F32)],
        scratch_shapes=[pltpu.VMEM((D_MODEL, IN_TN), BF16)],
        compiler_params=_params(2),
        name="in_proj",
    )(hn_p, hn_s, w_in)


def _rotate(x, cos2, sin2):
    return x * cos2 + pltpu.roll(x, RET_DK // 2, 1) * sin2


def _group_norm_gate(y, g):
    yn = y * lax.rsqrt(jnp.mean(y * y, axis=-1, keepdims=True) + NORM_EPS)
    return (g * jax.nn.sigmoid(g)) * yn


def _ret_prompt_kernel(dec_ref, q_ref, k_ref, v_ref, g_ref, cos_ref, sin_ref, ga_ref, st_ref,
                       s_scr, din_scr, dq_scr, dk_scr):
    c = pl.program_id(0)
    C = RET_CHUNK

    @pl.when(c == 0)
    def _():
        ii = lax.broadcasted_iota(jnp.int32, (C, C), 0).astype(F32)
        jj = lax.broadcasted_iota(jnp.int32, (C, C), 1).astype(F32)
        diff = ii - jj
        for h in range(HEADS):
            lg = dec_ref[0, h]
            din_scr[h] = jnp.where(diff >= 0, jnp.exp(lg * jnp.maximum(diff, 0.0)), 0.0)
            dq_scr[h] = jnp.exp(lg * (ii + 1.0))
            dk_scr[h] = jnp.exp(lg * (C - 1.0 - ii))
        s_scr[...] = jnp.zeros_like(s_scr)

    cos2 = cos_ref[...]
    sin2 = sin_ref[...]
    for h in range(HEADS):
        qr = _rotate(q_ref[:, h * RET_DK:(h + 1) * RET_DK], cos2, sin2)
        kr = _rotate(k_ref[:, h * RET_DK:(h + 1) * RET_DK], cos2, sin2) * (RET_DK ** -0.5)
        vb = v_ref[:, h * RET_DV:(h + 1) * RET_DV].astype(BF16)
        att = lax.dot_general(qr.astype(BF16), kr.astype(BF16), NT_DIMS, preferred_element_type=F32)
        inner = jnp.dot((att * din_scr[h]).astype(BF16), vb, preferred_element_type=F32)
        s_old = s_scr[h]
        cross = jnp.dot((qr * dq_scr[h]).astype(BF16), s_old.astype(BF16), preferred_element_type=F32)
        kv = lax.dot_general((kr * dk_scr[h]).astype(BF16), vb, TN_DIMS, preferred_element_type=F32)
        s_scr[h] = dec_ref[1, h] * s_old + kv
        gated = _group_norm_gate(inner + cross, g_ref[:, h * RET_DV:(h + 1) * RET_DV])
        ga_ref[:, h * RET_DV:(h + 1) * RET_DV] = gated.astype(ga_ref.dtype)

    @pl.when(c == pl.num_programs(0) - 1)
    def _():
        st_ref[...] = s_scr[...]


def _ret_prompt(dec, z_p, cos2, sin2):
    C = RET_CHUNK
    grid_spec = pltpu.PrefetchScalarGridSpec(
        num_scalar_prefetch=1,
        grid=(SEQ // C,),
        in_specs=[pl.BlockSpec((C, HEADS * RET_DK), lambda c, d: (c, COL_QA // 1024)),
                  pl.BlockSpec((C, HEADS * RET_DK), lambda c, d: (c, COL_KA // 1024)),
                  pl.BlockSpec((C, HEADS * RET_DV), lambda c, d: (c, COL_VA // 2048)),
                  pl.BlockSpec((C, HEADS * RET_DV), lambda c, d: (c, COL_GA // 2048)),
                  pl.BlockSpec((C, RET_DK), lambda c, d: (c, 0)),
                  pl.BlockSpec((C, RET_DK), lambda c, d: (c, 0))],
        out_specs=[pl.BlockSpec((C, HEADS * RET_DV), lambda c, d: (c, 0)),
                   pl.BlockSpec((HEADS, RET_DK, RET_DV), lambda c, d: (0, 0, 0))],
        scratch_shapes=[pltpu.VMEM((HEADS, RET_DK, RET_DV), F32),
                        pltpu.VMEM((HEADS, C, C), F32),
                        pltpu.VMEM((HEADS, C, C), F32),
                        pltpu.VMEM((HEADS, C, C), F32)],
    )
    return pl.pallas_call(
        _ret_prompt_kernel,
        grid_spec=grid_spec,
        out_shape=[jax.ShapeDtypeStruct((SEQ, HEADS * RET_DV), BF16),
                   jax.ShapeDtypeStruct((HEADS, RET_DK, RET_DV), F32)],
        compiler_params=_params(1),
        name="retention_prompt",
    )(dec, z_p, z_p, z_p, z_p, cos2, sin2)


RS_TB = 8


def _ret_sample_kernel(dec_ref, q_ref, k_ref, v_ref, g_ref, cos_ref, sin_ref, s0_ref, ga_ref, s1_ref, y_scr):
    i = pl.program_id(0)
    rows = pl.ds(pl.multiple_of(i * RS_TB, RS_TB), RS_TB)
    cos2 = cos_ref[...]
    sin2 = sin_ref[...]
    eye = (lax.broadcasted_iota(jnp.int32, (RET_DK, RET_DK), 0)
           == lax.broadcasted_iota(jnp.int32, (RET_DK, RET_DK), 1))
    ones = jnp.ones((RET_DK, RET_DV), BF16)

    def lane_bcast_column(row):
        diag = jnp.where(eye, jnp.broadcast_to(row, (RET_DK, RET_DK)), 0.0).astype(BF16)
        return jnp.dot(diag, ones, preferred_element_type=F32)

    for h in range(HEADS):
        gamma = dec_ref[2, h]
        qr = _rotate(q_ref[rows, h * RET_DK:(h + 1) * RET_DK], cos2, sin2)
        kr = _rotate(k_ref[rows, h * RET_DK:(h + 1) * RET_DK], cos2, sin2) * (RET_DK ** -0.5)
        vh = v_ref[rows, h * RET_DV:(h + 1) * RET_DV]
        qk = jnp.sum(qr * kr, axis=-1, keepdims=True)
        inner = qk * vh
        for b in range(RS_TB):
            s_old = s0_ref[b, h]
            cross = jnp.sum(lane_bcast_column(qr[b:b + 1, :]) * s_old, axis=0, keepdims=True)
            s1_ref[b, h] = gamma * s_old + lane_bcast_column(kr[b:b + 1, :]) * vh[b:b + 1, :]
            y_scr[b:b + 1, h * RET_DV:(h + 1) * RET_DV] = inner[b:b + 1, :] + cross * gamma
    for h in range(HEADS):
        cols = slice(h * RET_DV, (h + 1) * RET_DV)
        ga_ref[:, cols] = _group_norm_gate(y_scr[:, cols], g_ref[rows, cols])


def _ret_sample(dec, z_s, cos2, sin2, state):
    full = lambda shape, col: pl.BlockSpec(shape, lambda i, d: (0, col))
    grid_spec = pltpu.PrefetchScalarGridSpec(
        num_scalar_prefetch=1,
        grid=(DEC_BATCH // RS_TB,),
        in_specs=[full((DEC_BATCH, HEADS * RET_DK), COL_QA // 1024),
                  full((DEC_BATCH, HEADS * RET_DK), COL_KA // 1024),
                  full((DEC_BATCH, HEADS * RET_DV), COL_VA // 2048),
                  full((DEC_BATCH, HEADS * RET_DV), COL_GA // 2048),
                  full((1, RET_DK), 0),
                  full((1, RET_DK), 0),
                  pl.BlockSpec((RS_TB, HEADS, RET_DK, RET_DV), lambda i, d: (i, 0, 0, 0))],
        out_specs=[pl.BlockSpec((RS_TB, HEADS * RET_DV), lambda i, d: (i, 0)),
                   pl.BlockSpec((RS_TB, HEADS, RET_DK, RET_DV), lambda i, d: (i, 0, 0, 0))],
        scratch_shapes=[pltpu.VMEM((RS_TB, HEADS * RET_DV), F32)],
    )
    return pl.pallas_call(
        _ret_sample_kernel,
        grid_spec=grid_spec,
        out_shape=[jax.ShapeDtypeStruct((DEC_BATCH, HEADS * RET_DV), F32),
                   jax.ShapeDtypeStruct((DEC_BATCH, HEADS, RET_DK, RET_DV), F32)],
        compiler_params=_params(1),
        name="retention_sample",
    )(dec, z_s, z_s, z_s, z_s, cos2, sin2, state)


N_BLK = SEQ // MOBA_BLOCK
BIAS_W = 128
AUG_W = MOBA_DH + BIAS_W


FL_KC = 4
FL_CH = FL_KC * MOBA_BLOCK
N_CHUNK = SEQ // FL_CH
LOG2E = 1.4426950408889634
Q_SCALE = (MOBA_DH ** -0.5) * LOG2E


def _moba_prep_kernel(q_ref, k_ref, v_ref, qa_ref, ka_ref, vt_ref, kr_ref, vr_ref, km_scr):
    i = pl.program_id(0)

    @pl.when(i == 0)
    def _():
        km_scr[...] = jnp.zeros_like(km_scr)

    blk = lax.broadcasted_iota(jnp.int32, (N_BLK, MOBA_BLOCK), 0)
    sub8 = lax.broadcasted_iota(jnp.int32, (8, MOBA_BLOCK), 0)
    lane = lax.broadcasted_iota(jnp.int32, (MOBA_BLOCK, BIAS_W), 1)
    onehot = jnp.where(lane == i, 1.0, 0.0).astype(BF16)
    pad = jnp.full((BIAS_W - N_BLK, MOBA_BLOCK), NEG_INF, F32)
    for h in range(HEADS):
        cols = slice(h * MOBA_DH, (h + 1) * MOBA_DH)
        q = q_ref[:, cols]
        k = k_ref[:, cols]
        s_t = lax.dot_general(km_scr[h], q, NT_DIMS, precision=lax.Precision.HIGHEST,
                              preferred_element_type=F32)
        groups = [s_t[r:r + 8, :] for r in range(0, N_BLK, 8)]
        ranks = [jnp.zeros((8, MOBA_BLOCK), F32) for _ in groups]
        for jp in range(N_BLK):
            sj = s_t[jp:jp + 1, :]
            is_past = jnp.where(jp < i, 1.0, 0.0)
            g0, r0 = divmod(jp, 8)
            for gi, grp in enumerate(groups):
                if gi < g0:
                    hit = jnp.where(sj > grp, is_past, 0.0)
                elif gi > g0:
                    hit = jnp.where(sj >= grp, is_past, 0.0)
                else:
                    hit = jnp.where(sub8 > r0, jnp.where(sj >= grp, is_past, 0.0),
                                    jnp.where(sj > grp, is_past, 0.0))
                ranks[gi] = ranks[gi] + hit
        rank = jnp.concatenate(ranks, axis=0)
        keep = ((blk < i) & (rank < MOBA_TOPK)) | (blk == i)
        bias_t = jnp.concatenate([jnp.where(keep, 0.0, NEG_INF), pad], axis=0)
        qa_ref[h, :, :MOBA_DH] = (q * Q_SCALE).astype(BF16)
        qa_ref[h, :, MOBA_DH:] = bias_t.T.astype(BF16)
        ka_ref[h, :, :MOBA_DH] = k.astype(BF16)
        ka_ref[h, :, MOBA_DH:] = onehot
        v = v_ref[:, cols]
        vt_ref[h] = v.T.astype(BF16)
        kr_ref[:, h, :] = k
        vr_ref[:, h, :] = v
        km_scr[h, pl.ds(i, 1), :] = jnp.mean(k, axis=0, keepdims=True)


def _moba_prep(z_p):
    col = lambda base: (lambda i: (i, base // HD))
    out_idx = lambda i: (0, i, 0)
    return pl.pallas_call(
        _moba_prep_kernel,
        grid=(N_BLK,),
        in_specs=[pl.BlockSpec((MOBA_BLOCK, HD), col(COL_QB)),
                  pl.BlockSpec((MOBA_BLOCK, HD), col(COL_KB)),
                  pl.BlockSpec((MOBA_BLOCK, HD), col(COL_VB))],
        out_specs=[pl.BlockSpec((HEADS, MOBA_BLOCK, AUG_W), out_idx),
                   pl.BlockSpec((HEADS, MOBA_BLOCK, AUG_W), out_idx),
                   pl.BlockSpec((HEADS, None, MOBA_DH, MOBA_BLOCK), lambda i: (0, i // FL_KC, 0, i % FL_KC)),
                   pl.BlockSpec((MOBA_BLOCK, HEADS, MOBA_DH), lambda i: (i, 0, 0)),
                   pl.BlockSpec((MOBA_BLOCK, HEADS, MOBA_DH), lambda i: (i, 0, 0))],
        out_shape=[jax.ShapeDtypeStruct((HEADS, SEQ, AUG_W), BF16),
                   jax.ShapeDtypeStruct((HEADS, SEQ, AUG_W), BF16),
                   jax.ShapeDtypeStruct((HEADS, N_CHUNK, MOBA_DH, FL_CH), BF16),
                   jax.ShapeDtypeStruct((SEQ, HEADS, MOBA_DH), F32),
                   jax.ShapeDtypeStruct((SEQ, HEADS, MOBA_DH), F32)],
        scratch_shapes=[pltpu.VMEM((HEADS, N_BLK, MOBA_DH), F32)],
        compiler_params=_params(1),
        name="moba_prep",
    )(z_p, z_p, z_p)


FL_HEADS = 2


def _moba_flash_kernel(q_ref, k_ref, vt_ref, g_ref, o_ref):
    i = pl.program_id(1)
    last = i // FL_KC

    def chunk(hh, c):
        rows = pl.ds(pl.multiple_of(c * FL_CH, FL_CH), FL_CH)
        s_t = lax.dot_general(k_ref[hh, rows, :], q_ref[hh], NT_DIMS, preferred_element_type=F32)
        return s_t, vt_ref[hh, c]

    def update(carry, s_t, vt):
        m, l, acc = carry
        m_new = jnp.maximum(m, jnp.max(s_t, axis=0, keepdims=True))
        alpha = jnp.exp2(m - m_new)
        p = jnp.exp2(s_t - m_new)
        l = alpha * l + jnp.sum(p, axis=0, keepdims=True)
        acc = alpha * acc + jnp.dot(vt, p.astype(BF16), preferred_element_type=F32)
        return m_new, l, acc

    kpos = last * FL_CH + lax.broadcasted_iota(jnp.int32, (FL_CH, MOBA_BLOCK), 0)
    qpos = i * MOBA_BLOCK + lax.broadcasted_iota(jnp.int32, (FL_CH, MOBA_BLOCK), 1)
    causal = kpos <= qpos
    carries = []
    for hh in range(FL_HEADS):
        s_t, vt = chunk(hh, last)
        s_t = jnp.where(causal, s_t, NEG_INF)
        m = jnp.max(s_t, axis=0, keepdims=True)
        p = jnp.exp2(s_t - m)
        l = jnp.sum(p, axis=0, keepdims=True)
        acc = jnp.dot(vt, p.astype(BF16), preferred_element_type=F32)
        carries.append((m, l, acc))

    def body(c, carries):
        return tuple(update(carries[hh], *chunk(hh, c)) for hh in range(FL_HEADS))

    carries = lax.fori_loop(0, last, body, tuple(carries))
    for hh in range(FL_HEADS):
        m, l, acc = carries[hh]
        cols = slice(hh * MOBA_DH, (hh + 1) * MOBA_DH)
        g = g_ref[:, cols]
        o_ref[:, cols] = ((g * jax.nn.sigmoid(g)) * (acc / l).T).astype(o_ref.dtype)


def _moba_flash(q_aug, k_aug, v_t, z_p):
    gw = FL_HEADS * MOBA_DH
    return pl.pallas_call(
        _moba_flash_kernel,
        grid=(HEADS // FL_HEADS, N_BLK),
        in_specs=[pl.BlockSpec((FL_HEADS, MOBA_BLOCK, AUG_W), lambda h, i: (h, i, 0)),
                  pl.BlockSpec((FL_HEADS, SEQ, AUG_W), lambda h, i: (h, 0, 0)),
                  pl.BlockSpec((FL_HEADS, N_CHUNK, MOBA_DH, FL_CH), lambda h, i: (h, 0, 0, 0)),
                  pl.BlockSpec((MOBA_BLOCK, gw), lambda h, i: (i, COL_GB // gw + h))],
        out_specs=pl.BlockSpec((MOBA_BLOCK, gw), lambda h, i: (i, h)),
        out_shape=jax.ShapeDtypeStruct((SEQ, HEADS * MOBA_DH), BF16),
        compiler_params=_params(2),
        name="moba_flash",
    )(q_aug, k_aug, v_t, z_p)


PAST_BLOCKS = PAST_LEN // MOBA_BLOCK


PAGE_ROWS = PAGE_SIZE * HEADS
PAGES_PER_BLOCK = MOBA_BLOCK // PAGE_SIZE


def _per_head_allreduce(x, op):
    acc = x[:, 0:128]
    for c in range(1, x.shape[1] // 128):
        acc = op(acc, x[:, c * 128:(c + 1) * 128])
    shift = 64
    while shift >= HEADS:
        acc = op(acc, pltpu.roll(acc, shift, 1))
        shift //= 2
    return acc


def _moba_sample_kernel(pt_ref, x_ref, *refs):
    k_refs = refs[:N_PAGES]
    v_refs = refs[N_PAGES:2 * N_PAGES]
    o_ref = refs[2 * N_PAGES]
    q16 = (x_ref[0:HEADS, :] * (MOBA_DH ** -0.5)).astype(BF16)
    k_new = x_ref[HEADS:2 * HEADS, :]
    v_new = x_ref[2 * HEADS:3 * HEADS, :]
    g = x_ref[3 * HEADS:4 * HEADS, :]
    sub = lax.broadcasted_iota(jnp.int32, (HEADS, PAGE_ROWS), 0)
    lane = lax.broadcasted_iota(jnp.int32, (HEADS, PAGE_ROWS), 1)
    own = sub == (lane % HEADS)
    own_new = sub[:, 0:128] == lane[:, 0:128]

    def page_logits(k16, mask):
        r = lax.dot_general(q16, k16, NT_DIMS, preferred_element_type=F32)
        return jnp.sum(jnp.where(mask, r, 0.0), axis=0, keepdims=True)

    rows = [page_logits(k_refs[p][...].astype(BF16), own) for p in range(N_PAGES)]
    halves = [jnp.concatenate(rows[r::PAGES_PER_BLOCK], axis=0) for r in range(PAGES_PER_BLOCK)]
    zeros_tail = jnp.zeros((128 - HEADS, MOBA_DH), F32)
    k_self = jnp.concatenate([k_new, zeros_tail], axis=0).astype(BF16)
    s_self = _per_head_allreduce(page_logits(k_self, own_new), jnp.add)

    bsum = _per_head_allreduce(functools.reduce(jnp.add, halves), jnp.add)
    blk = lax.broadcasted_iota(jnp.int32, bsum.shape, 0)
    rank = jnp.zeros(bsum.shape, F32)
    for jp in range(PAST_BLOCKS):
        bj = bsum[jp:jp + 1, :]
        beats = (bj > bsum) | ((bj == bsum) & (blk > jp))
        rank = rank + jnp.where(beats, 1.0, 0.0)
    keep = jnp.where(rank < MOBA_TOPK, 1.0, 0.0)
    keep_full = jnp.concatenate([keep] * (PAGE_ROWS // 128), axis=1) > 0.5
    masked = [jnp.where(keep_full, hv, NEG_INF) for hv in halves]
    m_blk = _per_head_allreduce(functools.reduce(jnp.maximum, masked), jnp.maximum)
    m = jnp.maximum(jnp.max(m_blk, axis=0, keepdims=True), s_self)
    m_full = jnp.concatenate([m] * (PAGE_ROWS // 128), axis=1)
    probs = [jnp.exp(mv - m_full) for mv in masked]
    p_self = jnp.exp(s_self - m)
    l = jnp.sum(_per_head_allreduce(functools.reduce(jnp.add, probs), jnp.add), axis=0, keepdims=True) + p_self
    inv = 1.0 / l
    inv_full = jnp.concatenate([inv] * (PAGE_ROWS // 128), axis=1)
    probs = [pv * inv_full for pv in probs]

    v_self = jnp.concatenate([v_new, zeros_tail], axis=0).astype(BF16)
    w_self = jnp.where(own_new, jnp.broadcast_to(p_self * inv, (HEADS, 128)), 0.0).astype(BF16)
    acc = jnp.dot(w_self, v_self, preferred_element_type=F32)
    for p in range(N_PAGES):
        prow = probs[p % PAGES_PER_BLOCK][p // PAGES_PER_BLOCK:p // PAGES_PER_BLOCK + 1, :]
        w = jnp.where(own, jnp.broadcast_to(prow, (HEADS, PAGE_ROWS)), 0.0).astype(BF16)
        acc = acc + jnp.dot(w, v_refs[p][...].astype(BF16), preferred_element_type=F32)
    o_ref[...] = (g * jax.nn.sigmoid(g)) * acc


def _moba_sample(page_table, x_s, cache_k, cache_v):
    page = lambda p: pl.BlockSpec((None, PAGE_ROWS, MOBA_DH), lambda b, pt: (pt[b, p], 0, 0))
    grid_spec = pltpu.PrefetchScalarGridSpec(
        num_scalar_prefetch=1,
        grid=(DEC_BATCH,),
        in_specs=([pl.BlockSpec((None, 4 * HEADS, MOBA_DH), lambda b, pt: (b, 0, 0))]
                  + [page(p) for p in range(N_PAGES)] + [page(p) for p in range(N_PAGES)]),
        out_specs=pl.BlockSpec((None, HEADS, MOBA_DH), lambda b, pt: (b, 0, 0)),
    )
    return pl.pallas_call(
        _moba_sample_kernel,
        grid_spec=grid_spec,
        out_shape=jax.ShapeDtypeStruct((DEC_BATCH, HEADS, MOBA_DH), F32),
        compiler_params=_params(1),
        name="moba_sample",
    )(page_table, x_s, *([cache_k] * N_PAGES), *([cache_v] * N_PAGES))


def _out_proj_kernel(ga_ref, gb_ref, zma_ref, zmb_ref, x_ref, bm_ref, wa_ref, wb_ref, wo_ref, gp_ref, y_ref):
    u_a = jnp.dot(ga_ref[...], wa_ref[...], preferred_element_type=F32)
    u_b = jnp.dot(gb_ref[...], wb_ref[...], preferred_element_type=F32)
    gate_a = jax.nn.sigmoid(zma_ref[...] + bm_ref[0:1, :])
    gate_b = jax.nn.sigmoid(zmb_ref[...] + bm_ref[1:2, :])
    merged = (gate_a * u_a + gate_b * u_b).astype(BF16)
    o = jnp.dot(merged, wo_ref[...], preferred_element_type=F32)
    on = o * lax.rsqrt(jnp.mean(o * o, axis=-1, keepdims=True) + NORM_EPS)
    y_ref[...] = x_ref[...] + on * gp_ref[...]


def _out_proj(ga, gb, z, x, b_merge, w_a, w_b, w_o, g_post, tile):
    rows = x.shape[0]
    resident = lambda shape: pl.BlockSpec(shape, lambda i: (0, 0), pipeline_mode=pl.Buffered(1))
    return pl.pallas_call(
        _out_proj_kernel,
        grid=(rows // tile,),
        in_specs=[pl.BlockSpec((tile, HEADS * RET_DV), lambda i: (i, 0)),
                  pl.BlockSpec((tile, HD), lambda i: (i, 0)),
                  pl.BlockSpec((tile, D_MODEL), lambda i: (i, COL_ZM // D_MODEL)),
                  pl.BlockSpec((tile, D_MODEL), lambda i: (i, COL_ZM // D_MODEL + 1)),
                  pl.BlockSpec((tile, D_MODEL), lambda i: (i, 0)),
                  resident((2, D_MODEL)),
                  resident((HEADS * RET_DV, D_MODEL)),
                  resident((HD, D_MODEL)),
                  resident((D_MODEL, D_MODEL)),
                  resident((1, D_MODEL))],
        out_specs=pl.BlockSpec((tile, D_MODEL), lambda i: (i, 0)),
        out_shape=jax.ShapeDtypeStruct((rows, D_MODEL), F32),
        compiler_params=_params(1),
        name="out_proj",
    )(ga, gb, z, z, x, b_merge, w_a, w_b, w_o, g_post)


def _rope_tables(pos):
    half = RET_DK // 2
    inv = 1.0 / (RET_ROPE_BASE ** jnp.linspace(0.0, 1.0, half, dtype=F32))
    ang = pos.astype(F32)[:, None] * inv[None, :]
    cos, sin = jnp.cos(ang), jnp.sin(ang)
    return jnp.concatenate([cos, cos], axis=-1), jnp.concatenate([-sin, sin], axis=-1)


def kernel(x_prompt, x_sample, state_ret, cache_k, cache_v, page_table, norm_pre, w_in,
           w_a_out, w_b_out, b_merge, w_out, norm_post):
    xp = x_prompt.reshape(SEQ, D_MODEL)
    xs = x_sample.reshape(DEC_BATCH, D_MODEL)
    g_pre = norm_pre.reshape(1, D_MODEL)
    g_post = norm_post.reshape(1, D_MODEL)

    log_g = jnp.log(1.0 - 2.0 ** (-5.0 - jnp.arange(HEADS, dtype=F32)))
    dec = jnp.stack([log_g, jnp.exp(log_g * RET_CHUNK), jnp.exp(log_g * 1.0)])
    cos_p, sin_p = _rope_tables(jnp.arange(SEQ, dtype=jnp.int32))
    cos_s, sin_s = _rope_tables(PAST_LEN + jnp.arange(1, dtype=jnp.int32))

    hn_p = _rmsnorm(xp, g_pre, 1024)
    hn_s = _rmsnorm(xs, g_pre, DEC_BATCH)
    z_p, z_s = _in_proj(hn_p, hn_s, w_in.reshape(D_MODEL, PROJ_W))

    ga_p, st_p = _ret_prompt(dec, z_p, cos_p, sin_p)
    ga_s, st_s = _ret_sample(dec, z_s, cos_s, sin_s,
                             state_ret.reshape(DEC_BATCH, HEADS, RET_DK, RET_DV))

    q_aug, k_aug, v_t, k_rows_p, v_rows_p = _moba_prep(z_p)
    gb_p = _moba_flash(q_aug, k_aug, v_t, z_p)
    n_phys = cache_k.shape[1]
    gb_s = _moba_sample(page_table, z_s[:, COL_QB:COL_ZM].reshape(DEC_BATCH, 4 * HEADS, MOBA_DH),
                        cache_k.reshape(n_phys, PAGE_ROWS, MOBA_DH), cache_v.reshape(n_phys, PAGE_ROWS, MOBA_DH))

    w_a = w_a_out.reshape(HEADS * RET_DV, D_MODEL).astype(BF16)
    w_b = w_b_out.reshape(HD, D_MODEL).astype(BF16)
    w_o = w_out.reshape(D_MODEL, D_MODEL).astype(BF16)
    bm = b_merge.reshape(2, D_MODEL)
    y_p = _out_proj(ga_p, gb_p, z_p, xp, bm, w_a, w_b, w_o, g_post, 256)
    y_s = _out_proj(ga_s.astype(BF16), gb_s.reshape(DEC_BATCH, HD).astype(BF16), z_s, xs, bm,
                    w_a, w_b, w_o, g_post, DEC_BATCH)

    return (y_p.reshape(1, SEQ, D_MODEL),
            y_s.reshape(DEC_BATCH, 1, D_MODEL),
            st_p.reshape(1, 1, HEADS, RET_DK, RET_DV),
            st_s.reshape(1, DEC_BATCH, HEADS, RET_DK, RET_DV),
            k_rows_p.reshape(1, 1, SEQ, HEADS, MOBA_DH),
            v_rows_p.reshape(1, 1, SEQ, HEADS, MOBA_DH),
            z_s[:, COL_KB:COL_VB].reshape(1, DEC_BATCH, 1, HEADS, MOBA_DH),
            z_s[:, COL_VB:COL_GB].reshape(1, DEC_BATCH, 1, HEADS, MOBA_DH))
```

```python
import functools

import jax
import jax.numpy as jnp
from jax import lax
from jax.experimental import pallas as pl
from jax.experimental.pallas import tpu as pltpu

F32 = jnp.float32
BF16 = jnp.bfloat16

D_MODEL = 2048
SEQ = 8192
DEC_BATCH = 128
PAST_LEN = 2048
PAGE_SIZE = 128
N_PAGES = PAST_LEN // PAGE_SIZE
HEADS = 8
RET_DK = 128
RET_DV = 256
RET_CHUNK = 128
RET_ROPE_BASE = 10000.0
MOBA_DH = 128
HD = HEADS * MOBA_DH
MOBA_BLOCK = 256
MOBA_TOPK = 3
NORM_EPS = 1e-6
NEG_INF = -1e30

COL_QA, COL_KA, COL_VA, COL_GA = 0, 1024, 2048, 4096
COL_QB, COL_KB, COL_VB, COL_GB, COL_ZM = 6144, 7168, 8192, 9216, 10240
PROJ_W = 14336

V7X_VMEM_LIMIT = 56 * 1024 * 1024

NT_DIMS = (((1,), (1,)), ((), ()))
TN_DIMS = (((0,), (0,)), ((), ()))


def _params(n_axes, vmem=V7X_VMEM_LIMIT):
    return pltpu.CompilerParams(dimension_semantics=("arbitrary",) * n_axes, vmem_limit_bytes=vmem)


def _rmsnorm_kernel(x_ref, g_ref, o_ref):
    x = x_ref[...]
    y = x * lax.rsqrt(jnp.mean(x * x, axis=-1, keepdims=True) + NORM_EPS)
    o_ref[...] = (y * g_ref[...]).astype(o_ref.dtype)


def _rmsnorm(x, gain, tile):
    rows = x.shape[0]
    return pl.pallas_call(
        _rmsnorm_kernel,
        grid=(rows // tile,),
        in_specs=[pl.BlockSpec((tile, D_MODEL), lambda i: (i, 0)),
                  pl.BlockSpec((1, D_MODEL), lambda i: (0, 0))],
        out_specs=pl.BlockSpec((tile, D_MODEL), lambda i: (i, 0)),
        out_shape=jax.ShapeDtypeStruct((rows, D_MODEL), BF16),
        compiler_params=_params(1),
        name="rmsnorm_pre",
    )(x, gain)


IN_TM = 1024
IN_TN = 1024
IN_MP = SEQ // IN_TM


def _in_proj_kernel(hp_ref, hs_ref, w_ref, zp_ref, zs_ref, wb_ref):
    m = pl.program_id(1)

    @pl.when(m == 0)
    def _():
        wb_ref[...] = w_ref[...].astype(BF16)

    @pl.when(m < IN_MP)
    def _():
        zp_ref[...] = jnp.dot(hp_ref[...], wb_ref[...], preferred_element_type=F32)

    @pl.when(m == IN_MP)
    def _():
        zs_ref[...] = jnp.dot(hs_ref[...], wb_ref[...], preferred_element_type=F32)


def _in_proj(hn_p, hn_s, w_in):
    last = IN_MP - 1
    return pl.pallas_call(
        _in_proj_kernel,
        grid=(PROJ_W // IN_TN, IN_MP + 1),
        in_specs=[pl.BlockSpec((IN_TM, D_MODEL), lambda n, m: (jnp.minimum(m, last), 0)),
                  pl.BlockSpec((DEC_BATCH, D_MODEL), lambda n, m: (0, 0)),
                  pl.BlockSpec((D_MODEL, IN_TN), lambda n, m: (0, n))],
        out_specs=[pl.BlockSpec((IN_TM, IN_TN), lambda n, m: (jnp.minimum(m, last), n)),
                   pl.BlockSpec((DEC_BATCH, IN_TN), lambda n, m: (0, n))],
        out_shape=[jax.ShapeDtypeStruct((SEQ, PROJ_W), F32),
                   jax.ShapeDtypeStruct((DEC_BATCH, PROJ_W), F32)],
        scratch_shapes=[pltpu.VMEM((D_MODEL, IN_TN), BF16)],
        compiler_params=_params(2),
        name="in_proj",
    )(hn_p, hn_s, w_in)


def _rotate(x, cos2, sin2):
    return x * cos2 + pltpu.roll(x, RET_DK // 2, 1) * sin2


def _group_norm_gate(y, g):
    yn = y * lax.rsqrt(jnp.mean(y * y, axis=-1, keepdims=True) + NORM_EPS)
    return (g * jax.nn.sigmoid(g)) * yn


def _ret_prompt_kernel(dec_ref, q_ref, k_ref, v_ref, g_ref, cos_ref, sin_ref, ga_ref, st_ref,
                       s_scr, din_scr, dq_scr, dk_scr):
    c = pl.program_id(0)
    C = RET_CHUNK

    @pl.when(c == 0)
    def _():
        ii = lax.broadcasted_iota(jnp.int32, (C, C), 0).astype(F32)
        jj = lax.broadcasted_iota(jnp.int32, (C, C), 1).astype(F32)
        diff = ii - jj
        for h in range(HEADS):
            lg = dec_ref[0, h]
            din_scr[h] = jnp.where(diff >= 0, jnp.exp(lg * jnp.maximum(diff, 0.0)), 0.0)
            dq_scr[h] = jnp.exp(lg * (ii + 1.0))
            dk_scr[h] = jnp.exp(lg * (C - 1.0 - ii))
        s_scr[...] = jnp.zeros_like(s_scr)

    cos2 = cos_ref[...]
    sin2 = sin_ref[...]
    for h in range(HEADS):
        qr = _rotate(q_ref[:, h * RET_DK:(h + 1) * RET_DK], cos2, sin2)
        kr = _rotate(k_ref[:, h * RET_DK:(h + 1) * RET_DK], cos2, sin2) * (RET_DK ** -0.5)
        vb = v_ref[:, h * RET_DV:(h + 1) * RET_DV].astype(BF16)
        att = lax.dot_general(qr.astype(BF16), kr.astype(BF16), NT_DIMS, preferred_element_type=F32)
        inner = jnp.dot((att * din_scr[h]).astype(BF16), vb, preferred_element_type=F32)
        s_old = s_scr[h]
        cross = jnp.dot((qr * dq_scr[h]).astype(BF16), s_old.astype(BF16), preferred_element_type=F32)
        kv = lax.dot_general((kr * dk_scr[h]).astype(BF16), vb, TN_DIMS, preferred_element_type=F32)
        s_scr[h] = dec_ref[1, h] * s_old + kv
        gated = _group_norm_gate(inner + cross, g_ref[:, h * RET_DV:(h + 1) * RET_DV])
        ga_ref[:, h * RET_DV:(h + 1) * RET_DV] = gated.astype(ga_ref.dtype)

    @pl.when(c == pl.num_programs(0) - 1)
    def _():
        st_ref[...] = s_scr[...]


def _ret_prompt(dec, z_p, cos2, sin2):
    C = RET_CHUNK
    grid_spec = pltpu.PrefetchScalarGridSpec(
        num_scalar_prefetch=1,
        grid=(SEQ // C,),
        in_specs=[pl.BlockSpec((C, HEADS * RET_DK), lambda c, d: (c, COL_QA // 1024)),
                  pl.BlockSpec((C, HEADS * RET_DK), lambda c, d: (c, COL_KA // 1024)),
                  pl.BlockSpec((C, HEADS * RET_DV), lambda c, d: (c, COL_VA // 2048)),
                  pl.BlockSpec((C, HEADS * RET_DV), lambda c, d: (c, COL_GA // 2048)),
                  pl.BlockSpec((C, RET_DK), lambda c, d: (c, 0)),
                  pl.BlockSpec((C, RET_DK), lambda c, d: (c, 0))],
        out_specs=[pl.BlockSpec((C, HEADS * RET_DV), lambda c, d: (c, 0)),
                   pl.BlockSpec((HEADS, RET_DK, RET_DV), lambda c, d: (0, 0, 0))],
        scratch_shapes=[pltpu.VMEM((HEADS, RET_DK, RET_DV), F32),
                        pltpu.VMEM((HEADS, C, C), F32),
                        pltpu.VMEM((HEADS, C, C), F32),
                        pltpu.VMEM((HEADS, C, C), F32)],
    )
    return pl.pallas_call(
        _ret_prompt_kernel,
        grid_spec=grid_spec,
        out_shape=[jax.ShapeDtypeStruct((SEQ, HEADS * RET_DV), BF16),
                   jax.ShapeDtypeStruct((HEADS, RET_DK, RET_DV), F32)],
        compiler_params=_params(1),
        name="retention_prompt",
    )(dec, z_p, z_p, z_p, z_p, cos2, sin2)


RS_TB = 8


def _ret_sample_kernel(dec_ref, q_ref, k_ref, v_ref, g_ref, cos_ref, sin_ref, s0_ref, ga_ref, s1_ref, y_scr):
    i = pl.program_id(0)
    rows = pl.ds(pl.multiple_of(i * RS_TB, RS_TB), RS_TB)
    cos2 = cos_ref[...]
    sin2 = sin_ref[...]
    eye = (lax.broadcasted_iota(jnp.int32, (RET_DK, RET_DK), 0)
           == lax.broadcasted_iota(jnp.int32, (RET_DK, RET_DK), 1))
    ones = jnp.ones((RET_DK, RET_DV), BF16)

    def lane_bcast_column(row):
        diag = jnp.where(eye, jnp.broadcast_to(row, (RET_DK, RET_DK)), 0.0).astype(BF16)
        return jnp.dot(diag, ones, preferred_element_type=F32)

    for h in range(HEADS):
        gamma = dec_ref[2, h]
        qr = _rotate(q_ref[rows, h * RET_DK:(h + 1) * RET_DK], cos2, sin2)
        kr = _rotate(k_ref[rows, h * RET_DK:(h + 1) * RET_DK], cos2, sin2) * (RET_DK ** -0.5)
        vh = v_ref[rows, h * RET_DV:(h + 1) * RET_DV]
        qk = jnp.sum(qr * kr, axis=-1, keepdims=True)
        inner = qk * vh
        for b in range(RS_TB):
            s_old = s0_ref[b, h]
            cross = jnp.sum(lane_bcast_column(qr[b:b + 1, :]) * s_old, axis=0, keepdims=True)
            s1_ref[b, h] = gamma * s_old + lane_bcast_column(kr[b:b + 1, :]) * vh[b:b + 1, :]
            y_scr[b:b + 1, h * RET_DV:(h + 1) * RET_DV] = inner[b:b + 1, :] + cross * gamma
    for h in range(HEADS):
        cols = slice(h * RET_DV, (h + 1) * RET_DV)
        ga_ref[:, cols] = _group_norm_gate(y_scr[:, cols], g_ref[rows, cols])


def _ret_sample(dec, z_s, cos2, sin2, state):
    full = lambda shape, col: pl.BlockSpec(shape, lambda i, d: (0, col))
    grid_spec = pltpu.PrefetchScalarGridSpec(
        num_scalar_prefetch=1,
        grid=(DEC_BATCH // RS_TB,),
        in_specs=[full((DEC_BATCH, HEADS * RET_DK), COL_QA // 1024),
                  full((DEC_BATCH, HEADS * RET_DK), COL_KA // 1024),
                  full((DEC_BATCH, HEADS * RET_DV), COL_VA // 2048),
                  full((DEC_BATCH, HEADS * RET_DV), COL_GA // 2048),
                  full((1, RET_DK), 0),
                  full((1, RET_DK), 0),
                  pl.BlockSpec((RS_TB, HEADS, RET_DK, RET_DV), lambda i, d: (i, 0, 0, 0))],
        out_specs=[pl.BlockSpec((RS_TB, HEADS * RET_DV), lambda i, d: (i, 0)),
                   pl.BlockSpec((RS_TB, HEADS, RET_DK, RET_DV), lambda i, d: (i, 0, 0, 0))],
        scratch_shapes=[pltpu.VMEM((RS_TB, HEADS * RET_DV), F32)],
    )
    return pl.pallas_call(
        _ret_sample_kernel,
        grid_spec=grid_spec,
        out_shape=[jax.ShapeDtypeStruct((DEC_BATCH, HEADS * RET_DV), F32),
                   jax.ShapeDtypeStruct((DEC_BATCH, HEADS, RET_DK, RET_DV), F32)],
        compiler_params=_params(1),
        name="retention_sample",
    )(dec, z_s, z_s, z_s, z_s, cos2, sin2, state)


N_BLK = SEQ // MOBA_BLOCK
BIAS_W = 128
AUG_W = MOBA_DH + BIAS_W


FL_KC = 4
FL_CH = FL_KC * MOBA_BLOCK
N_CHUNK = SEQ // FL_CH
LOG2E = 1.4426950408889634
Q_SCALE = (MOBA_DH ** -0.5) * LOG2E


def _moba_prep_kernel(q_ref, k_ref, v_ref, qa_ref, ka_ref, vt_ref, kr_ref, vr_ref, km_scr):
    i = pl.program_id(0)

    @pl.when(i == 0)
    def _():
        km_scr[...] = jnp.zeros_like(km_scr)

    blk = lax.broadcasted_iota(jnp.int32, (N_BLK, MOBA_BLOCK), 0)
    sub8 = lax.broadcasted_iota(jnp.int32, (8, MOBA_BLOCK), 0)
    lane = lax.broadcasted_iota(jnp.int32, (MOBA_BLOCK, BIAS_W), 1)
    onehot = jnp.where(lane == i, 1.0, 0.0).astype(BF16)
    pad = jnp.full((BIAS_W - N_BLK, MOBA_BLOCK), NEG_INF, F32)
    for h in range(HEADS):
        cols = slice(h * MOBA_DH, (h + 1) * MOBA_DH)
        q = q_ref[:, cols]
        k = k_ref[:, cols]
        s_t = lax.dot_general(km_scr[h], q, NT_DIMS, precision=lax.Precision.HIGHEST,
                              preferred_element_type=F32)
        groups = [s_t[r:r + 8, :] for r in range(0, N_BLK, 8)]
        ranks = [jnp.zeros((8, MOBA_BLOCK), F32) for _ in groups]
        for jp in range(N_BLK):
            sj = s_t[jp:jp + 1, :]
            is_past = jnp.where(jp < i, 1.0, 0.0)
            g0, r0 = divmod(jp, 8)
            for gi, grp in enumerate(groups):
                if gi < g0:
                    hit = jnp.where(sj > grp, is_past, 0.0)
                elif gi > g0:
                    hit = jnp.where(sj >= grp, is_past, 0.0)
                else:
                    hit = jnp.where(sub8 > r0, jnp.where(sj >= grp, is_past, 0.0),
                                    jnp.where(sj > grp, is_past, 0.0))
                ranks[gi] = ranks[gi] + hit
        rank = jnp.concatenate(ranks, axis=0)
        keep = ((blk < i) & (rank < MOBA_TOPK)) | (blk == i)
        bias_t = jnp.concatenate([jnp.where(keep, 0.0, NEG_INF), pad], axis=0)
        qa_ref[h, :, :MOBA_DH] = (q * Q_SCALE).astype(BF16)
        qa_ref[h, :, MOBA_DH:] = bias_t.T.astype(BF16)
        ka_ref[h, :, :MOBA_DH] = k.astype(BF16)
        ka_ref[h, :, MOBA_DH:] = onehot
        v = v_ref[:, cols]
        vt_ref[h] = v.T.astype(BF16)
        kr_ref[:, h, :] = k
        vr_ref[:, h, :] = v
        km_scr[h, pl.ds(i, 1), :] = jnp.mean(k, axis=0, keepdims=True)


def _moba_prep(z_p):
    col = lambda base: (lambda i: (i, base // HD))
    out_idx = lambda i: (0, i, 0)
    return pl.pallas_call(
        _moba_prep_kernel,
        grid=(N_BLK,),
        in_specs=[pl.BlockSpec((MOBA_BLOCK, HD), col(COL_QB)),
                  pl.BlockSpec((MOBA_BLOCK, HD), col(COL_KB)),
                  pl.BlockSpec((MOBA_BLOCK, HD), col(COL_VB))],
        out_specs=[pl.BlockSpec((HEADS, MOBA_BLOCK, AUG_W), out_idx),
                   pl.BlockSpec((HEADS, MOBA_BLOCK, AUG_W), out_idx),
                   pl.BlockSpec((HEADS, None, MOBA_DH, MOBA_BLOCK), lambda i: (0, i // FL_KC, 0, i % FL_KC)),
                   pl.BlockSpec((MOBA_BLOCK, HEADS, MOBA_DH), lambda i: (i, 0, 0)),
                   pl.BlockSpec((MOBA_BLOCK, HEADS, MOBA_DH), lambda i: (i, 0, 0))],
        out_shape=[jax.ShapeDtypeStruct((HEADS, SEQ, AUG_W), BF16),
                   jax.ShapeDtypeStruct((HEADS, SEQ, AUG_W), BF16),
                   jax.ShapeDtypeStruct((HEADS, N_CHUNK, MOBA_DH, FL_CH), BF16),
                   jax.ShapeDtypeStruct((SEQ, HEADS, MOBA_DH), F32),
                   jax.ShapeDtypeStruct((SEQ, HEADS, MOBA_DH), F32)],
        scratch_shapes=[pltpu.VMEM((HEADS, N_BLK, MOBA_DH), F32)],
        compiler_params=_params(1),
        name="moba_prep",
    )(z_p, z_p, z_p)


FL_HEADS = 2


def _moba_flash_kernel(q_ref, k_ref, vt_ref, g_ref, o_ref, sa_ref, sb_ref):
    i = pl.program_id(1)
    last = i // FL_KC
    heads = range(FL_HEADS)

    def logits(hh, c):
        rows = pl.ds(pl.multiple_of(c * FL_CH, FL_CH), FL_CH)
        return lax.dot_general(k_ref[hh, rows, :], q_ref[hh], NT_DIMS, preferred_element_type=F32)

    def update(carry, s_t, vt):
        m, l, acc = carry
        m_new = jnp.maximum(m, jnp.max(s_t, axis=0, keepdims=True))
        alpha = jnp.exp2(m - m_new)
        p = jnp.exp2(s_t - m_new)
        l = alpha * l + jnp.sum(p, axis=0, keepdims=True)
        acc = alpha * acc + jnp.dot(vt, p.astype(BF16), preferred_element_type=F32)
        return m_new, l, acc

    kpos = last * FL_CH + lax.broadcasted_iota(jnp.int32, (FL_CH, MOBA_BLOCK), 0)
    qpos = i * MOBA_BLOCK + lax.broadcasted_iota(jnp.int32, (FL_CH, MOBA_BLOCK), 1)
    causal = kpos <= qpos
    for hh in heads:
        sa_ref[hh] = logits(hh, 0)
    carries = []
    for hh in heads:
        s_t = jnp.where(causal, logits(hh, last), NEG_INF)
        m = jnp.max(s_t, axis=0, keepdims=True)
        p = jnp.exp2(s_t - m)
        l = jnp.sum(p, axis=0, keepdims=True)
        acc = jnp.dot(vt_ref[hh, last], p.astype(BF16), preferred_element_type=F32)
        carries.append((m, l, acc))

    def consume(carries, s_ref, c):
        return tuple(update(carries[hh], s_ref[hh], vt_ref[hh, c]) for hh in heads)

    def pair(t, carries):
        c = 2 * t
        for hh in heads:
            sb_ref[hh] = logits(hh, c + 1)
        carries = consume(carries, sa_ref, c)
        for hh in heads:
            sa_ref[hh] = logits(hh, c + 2)
        return consume(carries, sb_ref, c + 1)

    carries = lax.fori_loop(0, last // 2, pair, tuple(carries))
    carries = lax.cond(last % 2 == 1, lambda cs: consume(cs, sa_ref, last - 1), lambda cs: cs, carries)
    for hh in heads:
        m, l, acc = carries[hh]
        cols = slice(hh * MOBA_DH, (hh + 1) * MOBA_DH)
        g = g_ref[:, cols]
        o_ref[:, cols] = ((g * jax.nn.sigmoid(g)) * (acc / l).T).astype(o_ref.dtype)


def _moba_flash(q_aug, k_aug, v_t, z_p):
    gw = FL_HEADS * MOBA_DH
    return pl.pallas_call(
        _moba_flash_kernel,
        grid=(HEADS // FL_HEADS, N_BLK),
        in_specs=[pl.BlockSpec((FL_HEADS, MOBA_BLOCK, AUG_W), lambda h, i: (h, i, 0)),
                  pl.BlockSpec((FL_HEADS, SEQ, AUG_W), lambda h, i: (h, 0, 0)),
                  pl.BlockSpec((FL_HEADS, N_CHUNK, MOBA_DH, FL_CH), lambda h, i: (h, 0, 0, 0)),
                  pl.BlockSpec((MOBA_BLOCK, gw), lambda h, i: (i, COL_GB // gw + h))],
        out_specs=pl.BlockSpec((MOBA_BLOCK, gw), lambda h, i: (i, h)),
        out_shape=jax.ShapeDtypeStruct((SEQ, HEADS * MOBA_DH), BF16),
        scratch_shapes=[pltpu.VMEM((FL_HEADS, FL_CH, MOBA_BLOCK), F32),
                        pltpu.VMEM((FL_HEADS, FL_CH, MOBA_BLOCK), F32)],
        compiler_params=_params(2),
        name="moba_flash",
    )(q_aug, k_aug, v_t, z_p)


PAST_BLOCKS = PAST_LEN // MOBA_BLOCK


PAGE_ROWS = PAGE_SIZE * HEADS
PAGES_PER_BLOCK = MOBA_BLOCK // PAGE_SIZE


def _per_head_allreduce(x, op):
    acc = x[:, 0:128]
    for c in range(1, x.shape[1] // 128):
        acc = op(acc, x[:, c * 128:(c + 1) * 128])
    shift = 64
    while shift >= HEADS:
        acc = op(acc, pltpu.roll(acc, shift, 1))
        shift //= 2
    return acc


def _moba_sample_kernel(pt_ref, x_ref, *refs):
    k_refs = refs[:N_PAGES]
    v_refs = refs[N_PAGES:2 * N_PAGES]
    o_ref = refs[2 * N_PAGES]
    q16 = (x_ref[0:HEADS, :] * (MOBA_DH ** -0.5)).astype(BF16)
    k_new = x_ref[HEADS:2 * HEADS, :]
    v_new = x_ref[2 * HEADS:3 * HEADS, :]
    g = x_ref[3 * HEADS:4 * HEADS, :]
    sub = lax.broadcasted_iota(jnp.int32, (HEADS, PAGE_ROWS), 0)
    lane = lax.broadcasted_iota(jnp.int32, (HEADS, PAGE_ROWS), 1)
    own = sub == (lane % HEADS)
    own_new = sub[:, 0:128] == lane[:, 0:128]

    def page_logits(k16, mask):
        r = lax.dot_general(q16, k16, NT_DIMS, preferred_element_type=F32)
        return jnp.sum(jnp.where(mask, r, 0.0), axis=0, keepdims=True)

    rows = [page_logits(k_refs[p][...].astype(BF16), own) for p in range(N_PAGES)]
    halves = [jnp.concatenate(rows[r::PAGES_PER_BLOCK], axis=0) for r in range(PAGES_PER_BLOCK)]
    zeros_tail = jnp.zeros((128 - HEADS, MOBA_DH), F32)
    k_self = jnp.concatenate([k_new, zeros_tail], axis=0).astype(BF16)
    s_self = _per_head_allreduce(page_logits(k_self, own_new), jnp.add)

    bsum = _per_head_allreduce(functools.reduce(jnp.add, halves), jnp.add)
    blk = lax.broadcasted_iota(jnp.int32, bsum.shape, 0)
    rank = jnp.zeros(bsum.shape, F32)
    for jp in range(PAST_BLOCKS):
        bj = bsum[jp:jp + 1, :]
        beats = (bj > bsum) | ((bj == bsum) & (blk > jp))
        rank = rank + jnp.where(beats, 1.0, 0.0)
    keep = jnp.where(rank < MOBA_TOPK, 1.0, 0.0)
    keep_full = jnp.concatenate([keep] * (PAGE_ROWS // 128), axis=1) > 0.5
    masked = [jnp.where(keep_full, hv, NEG_INF) for hv in halves]
    m_blk = _per_head_allreduce(functools.reduce(jnp.maximum, masked), jnp.maximum)
    m = jnp.maximum(jnp.max(m_blk, axis=0, keepdims=True), s_self)
    m_full = jnp.concatenate([m] * (PAGE_ROWS // 128), axis=1)
    probs = [jnp.exp(mv - m_full) for mv in masked]
    p_self = jnp.exp(s_self - m)
    l = jnp.sum(_per_head_allreduce(functools.reduce(jnp.add, probs), jnp.add), axis=0, keepdims=True) + p_self
    inv = 1.0 / l
    inv_full = jnp.concatenate([inv] * (PAGE_ROWS // 128), axis=1)
    probs = [pv * inv_full for pv in probs]

    v_self = jnp.concatenate([v_new, zeros_tail], axis=0).astype(BF16)
    w_self = jnp.where(own_new, jnp.broadcast_to(p_self * inv, (HEADS, 128)), 0.0).astype(BF16)
    acc = jnp.dot(w_self, v_self, preferred_element_type=F32)
    for p in range(N_PAGES):
        prow = probs[p % PAGES_PER_BLOCK][p // PAGES_PER_BLOCK:p // PAGES_PER_BLOCK + 1, :]
        w = jnp.where(own, jnp.broadcast_to(prow, (HEADS, PAGE_ROWS)), 0.0).astype(BF16)
        acc = acc + jnp.dot(w, v_refs[p][...].astype(BF16), preferred_element_type=F32)
    o_ref[...] = (g * jax.nn.sigmoid(g)) * acc


def _moba_sample(page_table, x_s, cache_k, cache_v):
    page = lambda p: pl.BlockSpec((None, PAGE_ROWS, MOBA_DH), lambda b, pt: (pt[b, p], 0, 0))
    grid_spec = pltpu.PrefetchScalarGridSpec(
        num_scalar_prefetch=1,
        grid=(DEC_BATCH,),
        in_specs=([pl.BlockSpec((None, 4 * HEADS, MOBA_DH), lambda b, pt: (b, 0, 0))]
                  + [page(p) for p in range(N_PAGES)] + [page(p) for p in range(N_PAGES)]),
        out_specs=pl.BlockSpec((None, HEADS, MOBA_DH), lambda b, pt: (b, 0, 0)),
    )
    return pl.pallas_call(
        _moba_sample_kernel,
        grid_spec=grid_spec,
        out_shape=jax.ShapeDtypeStruct((DEC_BATCH, HEADS, MOBA_DH), F32),
        compiler_params=_params(1),
        name="moba_sample",
    )(page_table, x_s, *([cache_k] * N_PAGES), *([cache_v] * N_PAGES))


def _out_proj_kernel(ga_ref, gb_ref, zma_ref, zmb_ref, x_ref, bm_ref, wa_ref, wb_ref, wo_ref, gp_ref, y_ref):
    u_a = jnp.dot(ga_ref[...], wa_ref[...], preferred_element_type=F32)
    u_b = jnp.dot(gb_ref[...], wb_ref[...], preferred_element_type=F32)
    gate_a = jax.nn.sigmoid(zma_ref[...] + bm_ref[0:1, :])
    gate_b = jax.nn.sigmoid(zmb_ref[...] + bm_ref[1:2, :])
    merged = (gate_a * u_a + gate_b * u_b).astype(BF16)
    o = jnp.dot(merged, wo_ref[...], preferred_element_type=F32)
    on = o * lax.rsqrt(jnp.mean(o * o, axis=-1, keepdims=True) + NORM_EPS)
    y_ref[...] = x_ref[...] + on * gp_ref[...]


def _out_proj(ga, gb, z, x, b_merge, w_a, w_b, w_o, g_post, tile):
    rows = x.shape[0]
    resident = lambda shape: pl.BlockSpec(shape, lambda i: (0, 0), pipeline_mode=pl.Buffered(1))
    return pl.pallas_call(
        _out_proj_kernel,
        grid=(rows // tile,),
        in_specs=[pl.BlockSpec((tile, HEADS * RET_DV), lambda i: (i, 0)),
                  pl.BlockSpec((tile, HD), lambda i: (i, 0)),
                  pl.BlockSpec((tile, D_MODEL), lambda i: (i, COL_ZM // D_MODEL)),
                  pl.BlockSpec((tile, D_MODEL), lambda i: (i, COL_ZM // D_MODEL + 1)),
                  pl.BlockSpec((tile, D_MODEL), lambda i: (i, 0)),
                  resident((2, D_MODEL)),
                  resident((HEADS * RET_DV, D_MODEL)),
                  resident((HD, D_MODEL)),
                  resident((D_MODEL, D_MODEL)),
                  resident((1, D_MODEL))],
        out_specs=pl.BlockSpec((tile, D_MODEL), lambda i: (i, 0)),
        out_shape=jax.ShapeDtypeStruct((rows, D_MODEL), F32),
        compiler_params=_params(1),
        name="out_proj",
    )(ga, gb, z, z, x, b_merge, w_a, w_b, w_o, g_post)


def _rope_tables(pos):
    half = RET_DK // 2
    inv = 1.0 / (RET_ROPE_BASE ** jnp.linspace(0.0, 1.0, half, dtype=F32))
    ang = pos.astype(F32)[:, None] * inv[None, :]
    cos, sin = jnp.cos(ang), jnp.sin(ang)
    return jnp.concatenate([cos, cos], axis=-1), jnp.concatenate([-sin, sin], axis=-1)


def kernel(x_prompt, x_sample, state_ret, cache_k, cache_v, page_table, norm_pre, w_in,
           w_a_out, w_b_out, b_merge, w_out, norm_post):
    xp = x_prompt.reshape(SEQ, D_MODEL)
    xs = x_sample.reshape(DEC_BATCH, D_MODEL)
    g_pre = norm_pre.reshape(1, D_MODEL)
    g_post = norm_post.reshape(1, D_MODEL)

    log_g = jnp.log(1.0 - 2.0 ** (-5.0 - jnp.arange(HEADS, dtype=F32)))
    dec = jnp.stack([log_g, jnp.exp(log_g * RET_CHUNK), jnp.exp(log_g * 1.0)])
    cos_p, sin_p = _rope_tables(jnp.arange(SEQ, dtype=jnp.int32))
    cos_s, sin_s = _rope_tables(PAST_LEN + jnp.arange(1, dtype=jnp.int32))

    hn_p = _rmsnorm(xp, g_pre, 1024)
    hn_s = _rmsnorm(xs, g_pre, DEC_BATCH)
    z_p, z_s = _in_proj(hn_p, hn_s, w_in.reshape(D_MODEL, PROJ_W))

    ga_p, st_p = _ret_prompt(dec, z_p, cos_p, sin_p)
    ga_s, st_s = _ret_sample(dec, z_s, cos_s, sin_s,
                             state_ret.reshape(DEC_BATCH, HEADS, RET_DK, RET_DV))

    q_aug, k_aug, v_t, k_rows_p, v_rows_p = _moba_prep(z_p)
    gb_p = _moba_flash(q_aug, k_aug, v_t, z_p)
    n_phys = cache_k.shape[1]
    gb_s = _moba_sample(page_table, z_s[:, COL_QB:COL_ZM].reshape(DEC_BATCH, 4 * HEADS, MOBA_DH),
                        cache_k.reshape(n_phys, PAGE_ROWS, MOBA_DH), cache_v.reshape(n_phys, PAGE_ROWS, MOBA_DH))

    w_a = w_a_out.reshape(HEADS * RET_DV, D_MODEL).astype(BF16)
    w_b = w_b_out.reshape(HD, D_MODEL).astype(BF16)
    w_o = w_out.reshape(D_MODEL, D_MODEL).astype(BF16)
    bm = b_merge.reshape(2, D_MODEL)
    y_p = _out_proj(ga_p, gb_p, z_p, xp, bm, w_a, w_b, w_o, g_post, 256)
    y_s = _out_proj(ga_s.astype(BF16), gb_s.reshape(DEC_BATCH, HD).astype(BF16), z_s, xs, bm,
                    w_a, w_b, w_o, g_post, DEC_BATCH)

    return (y_p.reshape(1, SEQ, D_MODEL),
            y_s.reshape(DEC_BATCH, 1, D_MODEL),
            st_p.reshape(1, 1, HEADS, RET_DK, RET_DV),
            st_s.reshape(1, DEC_BATCH, HEADS, RET_DK, RET_DV),
            k_rows_p.reshape(1, 1, SEQ, HEADS, MOBA_DH),
            v_rows_p.reshape(1, 1, SEQ, HEADS, MOBA_DH),
            z_s[:, COL_KB:COL_VB].reshape(1, DEC_BATCH, 1, HEADS, MOBA_DH),
            z_s[:, COL_VB:COL_GB].reshape(1, DEC_BATCH, 1, HEADS, MOBA_DH))
```

```python
import functools

import jax
import jax.numpy as jnp
from jax import lax
from jax.experimental import pallas as pl
from jax.experimental.pallas import tpu as pltpu

F32 = jnp.float32
BF16 = jnp.bfloat16

D_MODEL = 2048
SEQ = 8192
DEC_BATCH = 128
PAST_LEN = 2048
PAGE_SIZE = 128
N_PAGES = PAST_LEN // PAGE_SIZE
HEADS = 8
RET_DK = 128
RET_DV = 256
RET_CHUNK = 128
RET_ROPE_BASE = 10000.0
MOBA_DH = 128
HD = HEADS * MOBA_DH
MOBA_BLOCK = 256
MOBA_TOPK = 3
NORM_EPS = 1e-6
NEG_INF = -1e30

COL_QA, COL_KA, COL_VA, COL_GA = 0, 1024, 2048, 4096
COL_QB, COL_KB, COL_VB, COL_GB, COL_ZM = 6144, 7168, 8192, 9216, 10240
PROJ_W = 14336

V7X_VMEM_LIMIT = 56 * 1024 * 1024

NT_DIMS = (((1,), (1,)), ((), ()))
TN_DIMS = (((0,), (0,)), ((), ()))


def _params(n_axes, vmem=V7X_VMEM_LIMIT):
    return pltpu.CompilerParams(dimension_semantics=("arbitrary",) * n_axes, vmem_limit_bytes=vmem)


def _rmsnorm_kernel(x_ref, g_ref, o_ref):
    x = x_ref[...]
    y = x * lax.rsqrt(jnp.mean(x * x, axis=-1, keepdims=True) + NORM_EPS)
    o_ref[...] = (y * g_ref[...]).astype(o_ref.dtype)


def _rmsnorm(x, gain, tile):
    rows = x.shape[0]
    return pl.pallas_call(
        _rmsnorm_kernel,
        grid=(rows // tile,),
        in_specs=[pl.BlockSpec((tile, D_MODEL), lambda i: (i, 0)),
                  pl.BlockSpec((1, D_MODEL), lambda i: (0, 0))],
        out_specs=pl.BlockSpec((tile, D_MODEL), lambda i: (i, 0)),
        out_shape=jax.ShapeDtypeStruct((rows, D_MODEL), BF16),
        compiler_params=_params(1),
        name="rmsnorm_pre",
    )(x, gain)


IN_TM = 1024
IN_TN = 1024
IN_MP = SEQ // IN_TM


def _in_proj_kernel(hp_ref, hs_ref, w_ref, zp_ref, zs_ref, wb_ref):
    m = pl.program_id(1)

    @pl.when(m == 0)
    def _():
        wb_ref[...] = w_ref[...].astype(BF16)

    @pl.when(m < IN_MP)
    def _():
        zp_ref[...] = jnp.dot(hp_ref[...], wb_ref[...], preferred_element_type=F32)

    @pl.when(m == IN_MP)
    def _():
        zs_ref[...] = jnp.dot(hs_ref[...], wb_ref[...], preferred_element_type=F32)


def _in_proj(hn_p, hn_s, w_in):
    last = IN_MP - 1
    return pl.pallas_call(
        _in_proj_kernel,
        grid=(PROJ_W // IN_TN, IN_MP + 1),
        in_specs=[pl.BlockSpec((IN_TM, D_MODEL), lambda n, m: (jnp.minimum(m, last), 0)),
                  pl.BlockSpec((DEC_BATCH, D_MODEL), lambda n, m: (0, 0)),
                  pl.BlockSpec((D_MODEL, IN_TN), lambda n, m: (0, n))],
        out_specs=[pl.BlockSpec((IN_TM, IN_TN), lambda n, m: (jnp.minimum(m, last), n)),
                   pl.BlockSpec((DEC_BATCH, IN_TN), lambda n, m: (0, n))],
        out_shape=[jax.ShapeDtypeStruct((SEQ, PROJ_W), F32),
                   jax.ShapeDtypeStruct((DEC_BATCH, PROJ_W), F32)],
        scratch_shapes=[pltpu.VMEM((D_MODEL, IN_TN), BF16)],
        compiler_params=_params(2),
        name="in_proj",
    )(hn_p, hn_s, w_in)


def _rotate(x, cos2, sin2):
    return x * cos2 + pltpu.roll(x, RET_DK // 2, 1) * sin2


def _group_norm_gate(y, g):
    yn = y * lax.rsqrt(jnp.mean(y * y, axis=-1, keepdims=True) + NORM_EPS)
    return (g * jax.nn.sigmoid(g)) * yn


def _ret_prompt_kernel(dec_ref, q_ref, k_ref, v_ref, g_ref, cos_ref, sin_ref, ga_ref, st_ref,
                       s_scr, din_scr, dq_scr, dk_scr):
    c = pl.program_id(0)
    C = RET_CHUNK

    @pl.when(c == 0)
    def _():
        ii = lax.broadcasted_iota(jnp.int32, (C, C), 0).astype(F32)
        jj = lax.broadcasted_iota(jnp.int32, (C, C), 1).astype(F32)
        diff = ii - jj
        for h in range(HEADS):
            lg = dec_ref[0, h]
            din_scr[h] = jnp.where(diff >= 0, jnp.exp(lg * jnp.maximum(diff, 0.0)), 0.0)
            dq_scr[h] = jnp.exp(lg * (ii + 1.0))
            dk_scr[h] = jnp.exp(lg * (C - 1.0 - ii))
        s_scr[...] = jnp.zeros_like(s_scr)

    cos2 = cos_ref[...]
    sin2 = sin_ref[...]
    for h in range(HEADS):
        qr = _rotate(q_ref[:, h * RET_DK:(h + 1) * RET_DK], cos2, sin2)
        kr = _rotate(k_ref[:, h * RET_DK:(h + 1) * RET_DK], cos2, sin2) * (RET_DK ** -0.5)
        vb = v_ref[:, h * RET_DV:(h + 1) * RET_DV].astype(BF16)
        att = lax.dot_general(qr.astype(BF16), kr.astype(BF16), NT_DIMS, preferred_element_type=F32)
        inner = jnp.dot((att * din_scr[h]).astype(BF16), vb, preferred_element_type=F32)
        s_old = s_scr[h]
        cross = jnp.dot((qr * dq_scr[h]).astype(BF16), s_old.astype(BF16), preferred_element_type=F32)
        kv = lax.dot_general((kr * dk_scr[h]).astype(BF16), vb, TN_DIMS, preferred_element_type=F32)
        s_scr[h] = dec_ref[1, h] * s_old + kv
        gated = _group_norm_gate(inner + cross, g_ref[:, h * RET_DV:(h + 1) * RET_DV])
        ga_ref[:, h * RET_DV:(h + 1) * RET_DV] = gated.astype(ga_ref.dtype)

    @pl.when(c == pl.num_programs(0) - 1)
    def _():
        st_ref[...] = s_scr[...]


def _ret_prompt(dec, z_p, cos2, sin2):
    C = RET_CHUNK
    grid_spec = pltpu.PrefetchScalarGridSpec(
        num_scalar_prefetch=1,
        grid=(SEQ // C,),
        in_specs=[pl.BlockSpec((C, HEADS * RET_DK), lambda c, d: (c, COL_QA // 1024)),
                  pl.BlockSpec((C, HEADS * RET_DK), lambda c, d: (c, COL_KA // 1024)),
                  pl.BlockSpec((C, HEADS * RET_DV), lambda c, d: (c, COL_VA // 2048)),
                  pl.BlockSpec((C, HEADS * RET_DV), lambda c, d: (c, COL_GA // 2048)),
                  pl.BlockSpec((C, RET_DK), lambda c, d: (c, 0)),
                  pl.BlockSpec((C, RET_DK), lambda c, d: (c, 0))],
        out_specs=[pl.BlockSpec((C, HEADS * RET_DV), lambda c, d: (c, 0)),
                   pl.BlockSpec((HEADS, RET_DK, RET_DV), lambda c, d: (0, 0, 0))],
        scratch_shapes=[pltpu.VMEM((HEADS, RET_DK, RET_DV), F32),
                        pltpu.VMEM((HEADS, C, C), F32),
                        pltpu.VMEM((HEADS, C, C), F32),
                        pltpu.VMEM((HEADS, C, C), F32)],
    )
    return pl.pallas_call(
        _ret_prompt_kernel,
        grid_spec=grid_spec,
        out_shape=[jax.ShapeDtypeStruct((SEQ, HEADS * RET_DV), BF16),
                   jax.ShapeDtypeStruct((HEADS, RET_DK, RET_DV), F32)],
        compiler_params=_params(1),
        name="retention_prompt",
    )(dec, z_p, z_p, z_p, z_p, cos2, sin2)


RS_TB = 8


def _ret_sample_kernel(dec_ref, q_ref, k_ref, v_ref, g_ref, cos_ref, sin_ref, s0_ref, ga_ref, s1_ref, y_scr):
    i = pl.program_id(0)
    rows = pl.ds(pl.multiple_of(i * RS_TB, RS_TB), RS_TB)
    cos2 = cos_ref[...]
    sin2 = sin_ref[...]
    eye = (lax.broadcasted_iota(jnp.int32, (RET_DK, RET_DK), 0)
           == lax.broadcasted_iota(jnp.int32, (RET_DK, RET_DK), 1))
    ones = jnp.ones((RET_DK, RET_DV), BF16)

    def lane_bcast_column(row):
        diag = jnp.where(eye, jnp.broadcast_to(row, (RET_DK, RET_DK)), 0.0).astype(BF16)
        return jnp.dot(diag, ones, preferred_element_type=F32)

    for h in range(HEADS):
        gamma = dec_ref[2, h]
        qr = _rotate(q_ref[rows, h * RET_DK:(h + 1) * RET_DK], cos2, sin2)
        kr = _rotate(k_ref[rows, h * RET_DK:(h + 1) * RET_DK], cos2, sin2) * (RET_DK ** -0.5)
        vh = v_ref[rows, h * RET_DV:(h + 1) * RET_DV]
        qk = jnp.sum(qr * kr, axis=-1, keepdims=True)
        inner = qk * vh
        for b in range(RS_TB):
            s_old = s0_ref[b, h]
            cross = jnp.sum(lane_bcast_column(qr[b:b + 1, :]) * s_old, axis=0, keepdims=True)
            s1_ref[b, h] = gamma * s_old + lane_bcast_column(kr[b:b + 1, :]) * vh[b:b + 1, :]
            y_scr[b:b + 1, h * RET_DV:(h + 1) * RET_DV] = inner[b:b + 1, :] + cross * gamma
    for h in range(HEADS):
        cols = slice(h * RET_DV, (h + 1) * RET_DV)
        ga_ref[:, cols] = _group_norm_gate(y_scr[:, cols], g_ref[rows, cols])


def _ret_sample(dec, z_s, cos2, sin2, state):
    full = lambda shape, col: pl.BlockSpec(shape, lambda i, d: (0, col))
    grid_spec = pltpu.PrefetchScalarGridSpec(
        num_scalar_prefetch=1,
        grid=(DEC_BATCH // RS_TB,),
        in_specs=[full((DEC_BATCH, HEADS * RET_DK), COL_QA // 1024),
                  full((DEC_BATCH, HEADS * RET_DK), COL_KA // 1024),
                  full((DEC_BATCH, HEADS * RET_DV), COL_VA // 2048),
                  full((DEC_BATCH, HEADS * RET_DV), COL_GA // 2048),
                  full((1, RET_DK), 0),
                  full((1, RET_DK), 0),
                  pl.BlockSpec((RS_TB, HEADS, RET_DK, RET_DV), lambda i, d: (i, 0, 0, 0))],
        out_specs=[pl.BlockSpec((RS_TB, HEADS * RET_DV), lambda i, d: (i, 0)),
                   pl.BlockSpec((RS_TB, HEADS, RET_DK, RET_DV), lambda i, d: (i, 0, 0, 0))],
        scratch_shapes=[pltpu.VMEM((RS_TB, HEADS * RET_DV), F32)],
    )
    return pl.pallas_call(
        _ret_sample_kernel,
        grid_spec=grid_spec,
        out_shape=[jax.ShapeDtypeStruct((DEC_BATCH, HEADS * RET_DV), F32),
                   jax.ShapeDtypeStruct((DEC_BATCH, HEADS, RET_DK, RET_DV), F32)],
        compiler_params=_params(1),
        name="retention_sample",
    )(dec, z_s, z_s, z_s, z_s, cos2, sin2, state)


N_BLK = SEQ // MOBA_BLOCK
BIAS_W = 128
AUG_W = MOBA_DH + BIAS_W


FL_KC = 4
FL_CH = FL_KC * MOBA_BLOCK
N_CHUNK = SEQ // FL_CH
LOG2E = 1.4426950408889634
Q_SCALE = (MOBA_DH ** -0.5) * LOG2E


def _moba_prep_kernel(q_ref, k_ref, v_ref, qa_ref, ka_ref, vt_ref, kr_ref, vr_ref, km_scr):
    i = pl.program_id(0)

    @pl.when(i == 0)
    def _():
        km_scr[...] = jnp.zeros_like(km_scr)

    blk = lax.broadcasted_iota(jnp.int32, (N_BLK, MOBA_BLOCK), 0)
    sub8 = lax.broadcasted_iota(jnp.int32, (8, MOBA_BLOCK), 0)
    lane = lax.broadcasted_iota(jnp.int32, (MOBA_BLOCK, BIAS_W), 1)
    onehot = jnp.where(lane == i, 1.0, 0.0).astype(BF16)
    pad = jnp.full((BIAS_W - N_BLK, MOBA_BLOCK), NEG_INF, F32)
    for h in range(HEADS):
        cols = slice(h * MOBA_DH, (h + 1) * MOBA_DH)
        q = q_ref[:, cols]
        k = k_ref[:, cols]
        s_t = lax.dot_general(km_scr[h], q, NT_DIMS, precision=lax.Precision.HIGHEST,
                              preferred_element_type=F32)
        groups = [s_t[r:r + 8, :] for r in range(0, N_BLK, 8)]
        ranks = [jnp.zeros((8, MOBA_BLOCK), F32) for _ in groups]
        for jp in range(N_BLK):
            sj = s_t[jp:jp + 1, :]
            is_past = jnp.where(jp < i, 1.0, 0.0)
            g0, r0 = divmod(jp, 8)
            for gi, grp in enumerate(groups):
                if gi < g0:
                    hit = jnp.where(sj > grp, is_past, 0.0)
                elif gi > g0:
                    hit = jnp.where(sj >= grp, is_past, 0.0)
                else:
                    hit = jnp.where(sub8 > r0, jnp.where(sj >= grp, is_past, 0.0),
                                    jnp.where(sj > grp, is_past, 0.0))
                ranks[gi] = ranks[gi] + hit
        rank = jnp.concatenate(ranks, axis=0)
        keep = ((blk < i) & (rank < MOBA_TOPK)) | (blk == i)
        bias_t = jnp.concatenate([jnp.where(keep, 0.0, NEG_INF), pad], axis=0)
        qa_ref[h, :, :MOBA_DH] = (q * Q_SCALE).astype(BF16)
        qa_ref[h, :, MOBA_DH:] = bias_t.T.astype(BF16)
        ka_ref[h, :, :MOBA_DH] = k.astype(BF16)
        ka_ref[h, :, MOBA_DH:] = onehot
        v = v_ref[:, cols]
        vt_ref[h] = v.T.astype(BF16)
        kr_ref[:, h, :] = k
        vr_ref[:, h, :] = v
        km_scr[h, pl.ds(i, 1), :] = jnp.mean(k, axis=0, keepdims=True)


def _moba_prep(z_p):
    col = lambda base: (lambda i: (i, base // HD))
    out_idx = lambda i: (0, i, 0)
    return pl.pallas_call(
        _moba_prep_kernel,
        grid=(N_BLK,),
        in_specs=[pl.BlockSpec((MOBA_BLOCK, HD), col(COL_QB)),
                  pl.BlockSpec((MOBA_BLOCK, HD), col(COL_KB)),
                  pl.BlockSpec((MOBA_BLOCK, HD), col(COL_VB))],
        out_specs=[pl.BlockSpec((HEADS, MOBA_BLOCK, AUG_W), out_idx),
                   pl.BlockSpec((HEADS, MOBA_BLOCK, AUG_W), out_idx),
                   pl.BlockSpec((HEADS, None, MOBA_DH, MOBA_BLOCK), lambda i: (0, i // FL_KC, 0, i % FL_KC)),
                   pl.BlockSpec((MOBA_BLOCK, HEADS, MOBA_DH), lambda i: (i, 0, 0)),
                   pl.BlockSpec((MOBA_BLOCK, HEADS, MOBA_DH), lambda i: (i, 0, 0))],
        out_shape=[jax.ShapeDtypeStruct((HEADS, SEQ, AUG_W), BF16),
                   jax.ShapeDtypeStruct((HEADS, SEQ, AUG_W), BF16),
                   jax.ShapeDtypeStruct((HEADS, N_CHUNK, MOBA_DH, FL_CH), BF16),
                   jax.ShapeDtypeStruct((SEQ, HEADS, MOBA_DH), F32),
                   jax.ShapeDtypeStruct((SEQ, HEADS, MOBA_DH), F32)],
        scratch_shapes=[pltpu.VMEM((HEADS, N_BLK, MOBA_DH), F32)],
        compiler_params=_params(1),
        name="moba_prep",
    )(z_p, z_p, z_p)


FL_HEADS = 2


def _moba_flash_kernel(q_ref, k_ref, vt_ref, g_ref, o_ref, sa_ref, sb_ref):
    i = pl.program_id(1)
    last = i // FL_KC
    heads = range(FL_HEADS)

    def logits(hh, c):
        rows = pl.ds(pl.multiple_of(c * FL_CH, FL_CH), FL_CH)
        return lax.dot_general(k_ref[hh, rows, :], q_ref[hh], NT_DIMS, preferred_element_type=F32)

    def update(carry, s_t, vt):
        m, l, acc = carry
        m_new = jnp.maximum(m, jnp.max(s_t, axis=0, keepdims=True))
        alpha = jnp.exp2(m - m_new)
        p = jnp.exp2(s_t - m_new)
        l = alpha * l + jnp.sum(p, axis=0, keepdims=True)
        acc = alpha * acc + jnp.dot(vt, p.astype(BF16), preferred_element_type=F32)
        return m_new, l, acc

    kpos = last * FL_CH + lax.broadcasted_iota(jnp.int32, (FL_CH, MOBA_BLOCK), 0)
    qpos = i * MOBA_BLOCK + lax.broadcasted_iota(jnp.int32, (FL_CH, MOBA_BLOCK), 1)
    causal = kpos <= qpos
    for hh in heads:
        sa_ref[hh] = logits(hh, 0)
    carries = []
    for hh in heads:
        s_t = jnp.where(causal, logits(hh, last), NEG_INF)
        m = jnp.max(s_t, axis=0, keepdims=True)
        p = jnp.exp2(s_t - m)
        l = jnp.sum(p, axis=0, keepdims=True)
        acc = jnp.dot(vt_ref[hh, last], p.astype(BF16), preferred_element_type=F32)
        carries.append((m, l, acc))

    def consume(carries, s_ref, c):
        return tuple(update(carries[hh], s_ref[hh], vt_ref[hh, c]) for hh in heads)

    def pair(t, carries):
        c = 2 * t
        for hh in heads:
            sb_ref[hh] = logits(hh, c + 1)
        carries = consume(carries, sa_ref, c)
        for hh in heads:
            sa_ref[hh] = logits(hh, c + 2)
        return consume(carries, sb_ref, c + 1)

    carries = lax.fori_loop(0, last // 2, pair, tuple(carries))
    carries = lax.cond(last % 2 == 1, lambda cs: consume(cs, sa_ref, last - 1), lambda cs: cs, carries)
    for hh in heads:
        m, l, acc = carries[hh]
        cols = slice(hh * MOBA_DH, (hh + 1) * MOBA_DH)
        g = g_ref[:, cols]
        o_ref[:, cols] = ((g * jax.nn.sigmoid(g)) * (acc / l).T).astype(o_ref.dtype)


PAST_BLOCKS = PAST_LEN // MOBA_BLOCK


PAGE_ROWS = PAGE_SIZE * HEADS
PAGES_PER_BLOCK = MOBA_BLOCK // PAGE_SIZE


def _per_head_allreduce(x, op):
    acc = x[:, 0:128]
    for c in range(1, x.shape[1] // 128):
        acc = op(acc, x[:, c * 128:(c + 1) * 128])
    shift = 64
    while shift >= HEADS:
        acc = op(acc, pltpu.roll(acc, shift, 1))
        shift //= 2
    return acc


def _moba_sample_kernel(pt_ref, x_ref, *refs):
    k_refs = refs[:N_PAGES]
    v_refs = refs[N_PAGES:2 * N_PAGES]
    o_ref = refs[2 * N_PAGES]
    q16 = (x_ref[0:HEADS, :] * (MOBA_DH ** -0.5)).astype(BF16)
    k_new = x_ref[HEADS:2 * HEADS, :]
    v_new = x_ref[2 * HEADS:3 * HEADS, :]
    g = x_ref[3 * HEADS:4 * HEADS, :]
    sub = lax.broadcasted_iota(jnp.int32, (HEADS, PAGE_ROWS), 0)
    lane = lax.broadcasted_iota(jnp.int32, (HEADS, PAGE_ROWS), 1)
    own = sub == (lane % HEADS)
    own_new = sub[:, 0:128] == lane[:, 0:128]

    def page_logits(k16, mask):
        r = lax.dot_general(q16, k16, NT_DIMS, preferred_element_type=F32)
        return jnp.sum(jnp.where(mask, r, 0.0), axis=0, keepdims=True)

    rows = [page_logits(k_refs[p][...].astype(BF16), own) for p in range(N_PAGES)]
    halves = [jnp.concatenate(rows[r::PAGES_PER_BLOCK], axis=0) for r in range(PAGES_PER_BLOCK)]
    zeros_tail = jnp.zeros((128 - HEADS, MOBA_DH), F32)
    k_self = jnp.concatenate([k_new, zeros_tail], axis=0).astype(BF16)
    s_self = _per_head_allreduce(page_logits(k_self, own_new), jnp.add)

    bsum = _per_head_allreduce(functools.reduce(jnp.add, halves), jnp.add)
    blk = lax.broadcasted_iota(jnp.int32, bsum.shape, 0)
    rank = jnp.zeros(bsum.shape, F32)
    for jp in range(PAST_BLOCKS):
        bj = bsum[jp:jp + 1, :]
        beats = (bj > bsum) | ((bj == bsum) & (blk > jp))
        rank = rank + jnp.where(beats, 1.0, 0.0)
    keep = jnp.where(rank < MOBA_TOPK, 1.0, 0.0)
    keep_full = jnp.concatenate([keep] * (PAGE_ROWS // 128), axis=1) > 0.5
    masked = [jnp.where(keep_full, hv, NEG_INF) for hv in halves]
    m_blk = _per_head_allreduce(functools.reduce(jnp.maximum, masked), jnp.maximum)
    m = jnp.maximum(jnp.max(m_blk, axis=0, keepdims=True), s_self)
    m_full = jnp.concatenate([m] * (PAGE_ROWS // 128), axis=1)
    probs = [jnp.exp(mv - m_full) for mv in masked]
    p_self = jnp.exp(s_self - m)
    l = jnp.sum(_per_head_allreduce(functools.reduce(jnp.add, probs), jnp.add), axis=0, keepdims=True) + p_self
    inv = 1.0 / l
    inv_full = jnp.concatenate([inv] * (PAGE_ROWS // 128), axis=1)
    probs = [pv * inv_full for pv in probs]

    v_self = jnp.concatenate([v_new, zeros_tail], axis=0).astype(BF16)
    w_self = jnp.where(own_new, jnp.broadcast_to(p_self * inv, (HEADS, 128)), 0.0).astype(BF16)
    acc = jnp.dot(w_self, v_self, preferred_element_type=F32)
    for p in range(N_PAGES):
        prow = probs[p % PAGES_PER_BLOCK][p // PAGES_PER_BLOCK:p // PAGES_PER_BLOCK + 1, :]
        w = jnp.where(own, jnp.broadcast_to(prow, (HEADS, PAGE_ROWS)), 0.0).astype(BF16)
        acc = acc + jnp.dot(w, v_refs[p][...].astype(BF16), preferred_element_type=F32)
    o_ref[...] = (g * jax.nn.sigmoid(g)) * acc


FL_STEPS_PER_GROUP = N_BLK
assert (HEADS // FL_HEADS) * FL_STEPS_PER_GROUP == DEC_BATCH


def _moba_fused_kernel(pt_ref, q_ref, k_ref, vt_ref, g_ref, x_ref, *refs):
    pages = refs[:2 * N_PAGES]
    o_ref, os_ref, sa_ref, sb_ref = refs[2 * N_PAGES:]
    _moba_sample_kernel(pt_ref, x_ref, *pages, os_ref)
    _moba_flash_kernel(q_ref, k_ref, vt_ref, g_ref, o_ref, sa_ref, sb_ref)


def _moba_fused(page_table, q_aug, k_aug, v_t, z_p, x_s, cache_k, cache_v):
    gw = FL_HEADS * MOBA_DH
    seq = lambda h, i: h * FL_STEPS_PER_GROUP + i
    resident = pl.Buffered(1)
    page = lambda p: pl.BlockSpec((None, PAGE_ROWS, MOBA_DH), lambda h, i, pt: (pt[seq(h, i), p], 0, 0))
    grid_spec = pltpu.PrefetchScalarGridSpec(
        num_scalar_prefetch=1,
        grid=(HEADS // FL_HEADS, N_BLK),
        in_specs=([pl.BlockSpec((FL_HEADS, MOBA_BLOCK, AUG_W), lambda h, i, pt: (h, i, 0)),
                   pl.BlockSpec((FL_HEADS, SEQ, AUG_W), lambda h, i, pt: (h, 0, 0), pipeline_mode=resident),
                   pl.BlockSpec((FL_HEADS, N_CHUNK, MOBA_DH, FL_CH), lambda h, i, pt: (h, 0, 0, 0),
                                pipeline_mode=resident),
                   pl.BlockSpec((MOBA_BLOCK, gw), lambda h, i, pt: (i, COL_GB // gw + h)),
                   pl.BlockSpec((None, 4 * HEADS, MOBA_DH), lambda h, i, pt: (seq(h, i), 0, 0))]
                  + [page(p) for p in range(N_PAGES)] + [page(p) for p in range(N_PAGES)]),
        out_specs=[pl.BlockSpec((MOBA_BLOCK, gw), lambda h, i, pt: (i, h)),
                   pl.BlockSpec((None, HEADS, MOBA_DH), lambda h, i, pt: (seq(h, i), 0, 0))],
        scratch_shapes=[pltpu.VMEM((FL_HEADS, FL_CH, MOBA_BLOCK), F32),
                        pltpu.VMEM((FL_HEADS, FL_CH, MOBA_BLOCK), F32)],
    )
    return pl.pallas_call(
        _moba_fused_kernel,
        grid_spec=grid_spec,
        out_shape=[jax.ShapeDtypeStruct((SEQ, HEADS * MOBA_DH), BF16),
                   jax.ShapeDtypeStruct((DEC_BATCH, HEADS, MOBA_DH), F32)],
        compiler_params=_params(2),
        name="moba_fused",
    )(page_table, q_aug, k_aug, v_t, z_p, x_s, *([cache_k] * N_PAGES), *([cache_v] * N_PAGES))


def _out_proj_kernel(ga_ref, gb_ref, zma_ref, zmb_ref, x_ref, bm_ref, wa_ref, wb_ref, wo_ref, gp_ref, y_ref):
    u_a = jnp.dot(ga_ref[...], wa_ref[...], preferred_element_type=F32)
    u_b = jnp.dot(gb_ref[...], wb_ref[...], preferred_element_type=F32)
    gate_a = jax.nn.sigmoid(zma_ref[...] + bm_ref[0:1, :])
    gate_b = jax.nn.sigmoid(zmb_ref[...] + bm_ref[1:2, :])
    merged = (gate_a * u_a + gate_b * u_b).astype(BF16)
    o = jnp.dot(merged, wo_ref[...], preferred_element_type=F32)
    on = o * lax.rsqrt(jnp.mean(o * o, axis=-1, keepdims=True) + NORM_EPS)
    y_ref[...] = x_ref[...] + on * gp_ref[...]


def _out_proj(ga, gb, z, x, b_merge, w_a, w_b, w_o, g_post, tile):
    rows = x.shape[0]
    resident = lambda shape: pl.BlockSpec(shape, lambda i: (0, 0), pipeline_mode=pl.Buffered(1))
    return pl.pallas_call(
        _out_proj_kernel,
        grid=(rows // tile,),
        in_specs=[pl.BlockSpec((tile, HEADS * RET_DV), lambda i: (i, 0)),
                  pl.BlockSpec((tile, HD), lambda i: (i, 0)),
                  pl.BlockSpec((tile, D_MODEL), lambda i: (i, COL_ZM // D_MODEL)),
                  pl.BlockSpec((tile, D_MODEL), lambda i: (i, COL_ZM // D_MODEL + 1)),
                  pl.BlockSpec((tile, D_MODEL), lambda i: (i, 0)),
                  resident((2, D_MODEL)),
                  resident((HEADS * RET_DV, D_MODEL)),
                  resident((HD, D_MODEL)),
                  resident((D_MODEL, D_MODEL)),
                  resident((1, D_MODEL))],
        out_specs=pl.BlockSpec((tile, D_MODEL), lambda i: (i, 0)),
        out_shape=jax.ShapeDtypeStruct((rows, D_MODEL), F32),
        compiler_params=_params(1),
        name="out_proj",
    )(ga, gb, z, z, x, b_merge, w_a, w_b, w_o, g_post)


def _rope_tables(pos):
    half = RET_DK // 2
    inv = 1.0 / (RET_ROPE_BASE ** jnp.linspace(0.0, 1.0, half, dtype=F32))
    ang = pos.astype(F32)[:, None] * inv[None, :]
    cos, sin = jnp.cos(ang), jnp.sin(ang)
    return jnp.concatenate([cos, cos], axis=-1), jnp.concatenate([-sin, sin], axis=-1)


def kernel(x_prompt, x_sample, state_ret, cache_k, cache_v, page_table, norm_pre, w_in,
           w_a_out, w_b_out, b_merge, w_out, norm_post):
    xp = x_prompt.reshape(SEQ, D_MODEL)
    xs = x_sample.reshape(DEC_BATCH, D_MODEL)
    g_pre = norm_pre.reshape(1, D_MODEL)
    g_post = norm_post.reshape(1, D_MODEL)

    log_g = jnp.log(1.0 - 2.0 ** (-5.0 - jnp.arange(HEADS, dtype=F32)))
    dec = jnp.stack([log_g, jnp.exp(log_g * RET_CHUNK), jnp.exp(log_g * 1.0)])
    cos_p, sin_p = _rope_tables(jnp.arange(SEQ, dtype=jnp.int32))
    cos_s, sin_s = _rope_tables(PAST_LEN + jnp.arange(1, dtype=jnp.int32))

    hn_p = _rmsnorm(xp, g_pre, 1024)
    hn_s = _rmsnorm(xs, g_pre, DEC_BATCH)
    z_p, z_s = _in_proj(hn_p, hn_s, w_in.reshape(D_MODEL, PROJ_W))

    ga_p, st_p = _ret_prompt(dec, z_p, cos_p, sin_p)
    ga_s, st_s = _ret_sample(dec, z_s, cos_s, sin_s,
                             state_ret.reshape(DEC_BATCH, HEADS, RET_DK, RET_DV))

    q_aug, k_aug, v_t, k_rows_p, v_rows_p = _moba_prep(z_p)
    n_phys = cache_k.shape[1]
    gb_p, gb_s = _moba_fused(page_table, q_aug, k_aug, v_t, z_p,
                             z_s[:, COL_QB:COL_ZM].reshape(DEC_BATCH, 4 * HEADS, MOBA_DH),
                             cache_k.reshape(n_phys, PAGE_ROWS, MOBA_DH),
                             cache_v.reshape(n_phys, PAGE_ROWS, MOBA_DH))

    w_a = w_a_out.reshape(HEADS * RET_DV, D_MODEL).astype(BF16)
    w_b = w_b_out.reshape(HD, D_MODEL).astype(BF16)
    w_o = w_out.reshape(D_MODEL, D_MODEL).astype(BF16)
    bm = b_merge.reshape(2, D_MODEL)
    y_p = _out_proj(ga_p, gb_p, z_p, xp, bm, w_a, w_b, w_o, g_post, 256)
    y_s = _out_proj(ga_s.astype(BF16), gb_s.reshape(DEC_BATCH, HD).astype(BF16), z_s, xs, bm,
                    w_a, w_b, w_o, g_post, DEC_BATCH)

    return (y_p.reshape(1, SEQ, D_MODEL),
            y_s.reshape(DEC_BATCH, 1, D_MODEL),
            st_p.reshape(1, 1, HEADS, RET_DK, RET_DV),
            st_s.reshape(1, DEC_BATCH, HEADS, RET_DK, RET_DV),
            k_rows_p.reshape(1, 1, SEQ, HEADS, MOBA_DH),
            v_rows_p.reshape(1, 1, SEQ, HEADS, MOBA_DH),
            z_s[:, COL_KB:COL_VB].reshape(1, DEC_BATCH, 1, HEADS, MOBA_DH),
            z_s[:, COL_VB:COL_GB].reshape(1, DEC_BATCH, 1, HEADS, MOBA_DH))
```

```python
import functools

import jax
import jax.numpy as jnp
from jax import lax
from jax.experimental import pallas as pl
from jax.experimental.pallas import tpu as pltpu

F32 = jnp.float32
BF16 = jnp.bfloat16

D_MODEL = 2048
SEQ = 8192
DEC_BATCH = 128
PAST_LEN = 2048
PAGE_SIZE = 128
N_PAGES = PAST_LEN // PAGE_SIZE
HEADS = 8
RET_DK = 128
RET_DV = 256
RET_CHUNK = 128
RET_ROPE_BASE = 10000.0
MOBA_DH = 128
HD = HEADS * MOBA_DH
MOBA_BLOCK = 256
MOBA_TOPK = 3
NORM_EPS = 1e-6
NEG_INF = -1e30

COL_QA, COL_KA, COL_VA, COL_GA = 0, 1024, 2048, 4096
COL_QB, COL_KB, COL_VB, COL_GB, COL_ZM = 6144, 7168, 8192, 9216, 10240
PROJ_W = 14336

V7X_VMEM_LIMIT = 56 * 1024 * 1024

NT_DIMS = (((1,), (1,)), ((), ()))
TN_DIMS = (((0,), (0,)), ((), ()))


def _params(n_axes, vmem=V7X_VMEM_LIMIT):
    return pltpu.CompilerParams(dimension_semantics=("arbitrary",) * n_axes, vmem_limit_bytes=vmem)


def _rmsnorm_kernel(x_ref, g_ref, o_ref):
    x = x_ref[...]
    y = x * lax.rsqrt(jnp.mean(x * x, axis=-1, keepdims=True) + NORM_EPS)
    o_ref[...] = (y * g_ref[...]).astype(o_ref.dtype)


def _rmsnorm(x, gain, tile):
    rows = x.shape[0]
    return pl.pallas_call(
        _rmsnorm_kernel,
        grid=(rows // tile,),
        in_specs=[pl.BlockSpec((tile, D_MODEL), lambda i: (i, 0)),
                  pl.BlockSpec((1, D_MODEL), lambda i: (0, 0))],
        out_specs=pl.BlockSpec((tile, D_MODEL), lambda i: (i, 0)),
        out_shape=jax.ShapeDtypeStruct((rows, D_MODEL), BF16),
        compiler_params=_params(1),
        name="rmsnorm_pre",
    )(x, gain)


IN_TM = 1024
IN_TN = 1024
IN_MP = SEQ // IN_TM


def _in_proj_kernel(hp_ref, hs_ref, w_ref, zp_ref, zs_ref, wb_ref):
    m = pl.program_id(1)

    @pl.when(m == 0)
    def _():
        wb_ref[...] = w_ref[...].astype(BF16)

    @pl.when(m < IN_MP)
    def _():
        zp_ref[...] = jnp.dot(hp_ref[...], wb_ref[...], preferred_element_type=F32)

    @pl.when(m == IN_MP)
    def _():
        zs_ref[...] = jnp.dot(hs_ref[...], wb_ref[...], preferred_element_type=F32)


def _in_proj(hn_p, hn_s, w_in):
    last = IN_MP - 1
    return pl.pallas_call(
        _in_proj_kernel,
        grid=(PROJ_W // IN_TN, IN_MP + 1),
        in_specs=[pl.BlockSpec((IN_TM, D_MODEL), lambda n, m: (jnp.minimum(m, last), 0)),
                  pl.BlockSpec((DEC_BATCH, D_MODEL), lambda n, m: (0, 0)),
                  pl.BlockSpec((D_MODEL, IN_TN), lambda n, m: (0, n))],
        out_specs=[pl.BlockSpec((IN_TM, IN_TN), lambda n, m: (jnp.minimum(m, last), n)),
                   pl.BlockSpec((DEC_BATCH, IN_TN), lambda n, m: (0, n))],
        out_shape=[jax.ShapeDtypeStruct((SEQ, PROJ_W), F32),
                   jax.ShapeDtypeStruct((DEC_BATCH, PROJ_W), F32)],
        scratch_shapes=[pltpu.VMEM((D_MODEL, IN_TN), BF16)],
        compiler_params=_params(2),
        name="in_proj",
    )(hn_p, hn_s, w_in)


def _rotate(x, cos2, sin2):
    return x * cos2 + pltpu.roll(x, RET_DK // 2, 1) * sin2


def _group_norm_gate(y, g):
    yn = y * lax.rsqrt(jnp.mean(y * y, axis=-1, keepdims=True) + NORM_EPS)
    return (g * jax.nn.sigmoid(g)) * yn


def _ret_prompt_kernel(dec_ref, q_ref, k_ref, v_ref, g_ref, cos_ref, sin_ref, ga_ref, st_ref,
                       s_scr, din_scr, dq_scr, dk_scr):
    c = pl.program_id(0)
    C = RET_CHUNK

    @pl.when(c == 0)
    def _():
        ii = lax.broadcasted_iota(jnp.int32, (C, C), 0).astype(F32)
        jj = lax.broadcasted_iota(jnp.int32, (C, C), 1).astype(F32)
        diff = ii - jj
        for h in range(HEADS):
            lg = dec_ref[0, h]
            din_scr[h] = jnp.where(diff >= 0, jnp.exp(lg * jnp.maximum(diff, 0.0)), 0.0)
            dq_scr[h] = jnp.exp(lg * (ii + 1.0))
            dk_scr[h] = jnp.exp(lg * (C - 1.0 - ii))
        s_scr[...] = jnp.zeros_like(s_scr)

    cos2 = cos_ref[...]
    sin2 = sin_ref[...]
    for h in range(HEADS):
        qr = _rotate(q_ref[:, h * RET_DK:(h + 1) * RET_DK], cos2, sin2)
        kr = _rotate(k_ref[:, h * RET_DK:(h + 1) * RET_DK], cos2, sin2) * (RET_DK ** -0.5)
        vb = v_ref[:, h * RET_DV:(h + 1) * RET_DV].astype(BF16)
        att = lax.dot_general(qr.astype(BF16), kr.astype(BF16), NT_DIMS, preferred_element_type=F32)
        inner = jnp.dot((att * din_scr[h]).astype(BF16), vb, preferred_element_type=F32)
        s_old = s_scr[h]
        cross = jnp.dot((qr * dq_scr[h]).astype(BF16), s_old.astype(BF16), preferred_element_type=F32)
        kv = lax.dot_general((kr * dk_scr[h]).astype(BF16), vb, TN_DIMS, preferred_element_type=F32)
        s_scr[h] = dec_ref[1, h] * s_old + kv
        gated = _group_norm_gate(inner + cross, g_ref[:, h * RET_DV:(h + 1) * RET_DV])
        ga_ref[:, h * RET_DV:(h + 1) * RET_DV] = gated.astype(ga_ref.dtype)

    @pl.when(c == pl.num_programs(0) - 1)
    def _():
        st_ref[...] = s_scr[...]


def _ret_prompt(dec, z_p, cos2, sin2):
    C = RET_CHUNK
    grid_spec = pltpu.PrefetchScalarGridSpec(
        num_scalar_prefetch=1,
        grid=(SEQ // C,),
        in_specs=[pl.BlockSpec((C, HEADS * RET_DK), lambda c, d: (c, COL_QA // 1024)),
                  pl.BlockSpec((C, HEADS * RET_DK), lambda c, d: (c, COL_KA // 1024)),
                  pl.BlockSpec((C, HEADS * RET_DV), lambda c, d: (c, COL_VA // 2048)),
                  pl.BlockSpec((C, HEADS * RET_DV), lambda c, d: (c, COL_GA // 2048)),
                  pl.BlockSpec((C, RET_DK), lambda c, d: (c, 0)),
                  pl.BlockSpec((C, RET_DK), lambda c, d: (c, 0))],
        out_specs=[pl.BlockSpec((C, HEADS * RET_DV), lambda c, d: (c, 0)),
                   pl.BlockSpec((HEADS, RET_DK, RET_DV), lambda c, d: (0, 0, 0))],
        scratch_shapes=[pltpu.VMEM((HEADS, RET_DK, RET_DV), F32),
                        pltpu.VMEM((HEADS, C, C), F32),
                        pltpu.VMEM((HEADS, C, C), F32),
                        pltpu.VMEM((HEADS, C, C), F32)],
    )
    return pl.pallas_call(
        _ret_prompt_kernel,
        grid_spec=grid_spec,
        out_shape=[jax.ShapeDtypeStruct((SEQ, HEADS * RET_DV), BF16),
                   jax.ShapeDtypeStruct((HEADS, RET_DK, RET_DV), F32)],
        compiler_params=_params(1),
        name="retention_prompt",
    )(dec, z_p, z_p, z_p, z_p, cos2, sin2)


RS_TB = 8


def _ret_sample_kernel(dec_ref, q_ref, k_ref, v_ref, g_ref, cos_ref, sin_ref, s0_ref, ga_ref, s1_ref, y_scr):
    i = pl.program_id(0)
    rows = pl.ds(pl.multiple_of(i * RS_TB, RS_TB), RS_TB)
    cos2 = cos_ref[...]
    sin2 = sin_ref[...]
    eye = (lax.broadcasted_iota(jnp.int32, (RET_DK, RET_DK), 0)
           == lax.broadcasted_iota(jnp.int32, (RET_DK, RET_DK), 1))
    ones = jnp.ones((RET_DK, RET_DV), BF16)

    def lane_bcast_column(row):
        diag = jnp.where(eye, jnp.broadcast_to(row, (RET_DK, RET_DK)), 0.0).astype(BF16)
        return jnp.dot(diag, ones, preferred_element_type=F32)

    for h in range(HEADS):
        gamma = dec_ref[2, h]
        qr = _rotate(q_ref[rows, h * RET_DK:(h + 1) * RET_DK], cos2, sin2)
        kr = _rotate(k_ref[rows, h * RET_DK:(h + 1) * RET_DK], cos2, sin2) * (RET_DK ** -0.5)
        vh = v_ref[rows, h * RET_DV:(h + 1) * RET_DV]
        qk = jnp.sum(qr * kr, axis=-1, keepdims=True)
        inner = qk * vh
        for b in range(RS_TB):
            s_old = s0_ref[b, h]
            cross = jnp.sum(lane_bcast_column(qr[b:b + 1, :]) * s_old, axis=0, keepdims=True)
            s1_ref[b, h] = gamma * s_old + lane_bcast_column(kr[b:b + 1, :]) * vh[b:b + 1, :]
            y_scr[b:b + 1, h * RET_DV:(h + 1) * RET_DV] = inner[b:b + 1, :] + cross * gamma
    for h in range(HEADS):
        cols = slice(h * RET_DV, (h + 1) * RET_DV)
        ga_ref[:, cols] = _group_norm_gate(y_scr[:, cols], g_ref[rows, cols])


def _ret_sample(dec, z_s, cos2, sin2, state):
    full = lambda shape, col: pl.BlockSpec(shape, lambda i, d: (0, col))
    grid_spec = pltpu.PrefetchScalarGridSpec(
        num_scalar_prefetch=1,
        grid=(DEC_BATCH // RS_TB,),
        in_specs=[full((DEC_BATCH, HEADS * RET_DK), COL_QA // 1024),
                  full((DEC_BATCH, HEADS * RET_DK), COL_KA // 1024),
                  full((DEC_BATCH, HEADS * RET_DV), COL_VA // 2048),
                  full((DEC_BATCH, HEADS * RET_DV), COL_GA // 2048),
                  full((1, RET_DK), 0),
                  full((1, RET_DK), 0),
                  pl.BlockSpec((RS_TB, HEADS, RET_DK, RET_DV), lambda i, d: (i, 0, 0, 0))],
        out_specs=[pl.BlockSpec((RS_TB, HEADS * RET_DV), lambda i, d: (i, 0)),
                   pl.BlockSpec((RS_TB, HEADS, RET_DK, RET_DV), lambda i, d: (i, 0, 0, 0))],
        scratch_shapes=[pltpu.VMEM((RS_TB, HEADS * RET_DV), F32)],
    )
    return pl.pallas_call(
        _ret_sample_kernel,
        grid_spec=grid_spec,
        out_shape=[jax.ShapeDtypeStruct((DEC_BATCH, HEADS * RET_DV), F32),
                   jax.ShapeDtypeStruct((DEC_BATCH, HEADS, RET_DK, RET_DV), F32)],
        compiler_params=_params(1),
        name="retention_sample",
    )(dec, z_s, z_s, z_s, z_s, cos2, sin2, state)


N_BLK = SEQ // MOBA_BLOCK
BIAS_W = 128
AUG_W = MOBA_DH + BIAS_W


FL_KC = 4
FL_CH = FL_KC * MOBA_BLOCK
N_CHUNK = SEQ // FL_CH
LOG2E = 1.4426950408889634
Q_SCALE = (MOBA_DH ** -0.5) * LOG2E


def _moba_prep_kernel(q_ref, k_ref, v_ref, qa_ref, ka_ref, vt_ref, kr_ref, vr_ref, km_scr):
    i = pl.program_id(0)

    @pl.when(i == 0)
    def _():
        km_scr[...] = jnp.zeros_like(km_scr)

    blk = lax.broadcasted_iota(jnp.int32, (N_BLK, MOBA_BLOCK), 0)
    sub8 = lax.broadcasted_iota(jnp.int32, (8, MOBA_BLOCK), 0)
    lane = lax.broadcasted_iota(jnp.int32, (MOBA_BLOCK, BIAS_W), 1)
    onehot = jnp.where(lane == i, 1.0, 0.0).astype(BF16)
    pad = jnp.full((BIAS_W - N_BLK, MOBA_BLOCK), NEG_INF, F32)
    for h in range(HEADS):
        cols = slice(h * MOBA_DH, (h + 1) * MOBA_DH)
        q = q_ref[:, cols]
        k = k_ref[:, cols]
        s_t = lax.dot_general(km_scr[h], q, NT_DIMS, precision=lax.Precision.HIGHEST,
                              preferred_element_type=F32)
        groups = [s_t[r:r + 8, :] for r in range(0, N_BLK, 8)]
        ranks = [jnp.zeros((8, MOBA_BLOCK), F32) for _ in groups]
        for jp in range(N_BLK):
            sj = s_t[jp:jp + 1, :]
            is_past = jnp.where(jp < i, 1.0, 0.0)
            g0, r0 = divmod(jp, 8)
            for gi, grp in enumerate(groups):
                if gi < g0:
                    hit = jnp.where(sj > grp, is_past, 0.0)
                elif gi > g0:
                    hit = jnp.where(sj >= grp, is_past, 0.0)
                else:
                    hit = jnp.where(sub8 > r0, jnp.where(sj >= grp, is_past, 0.0),
                                    jnp.where(sj > grp, is_past, 0.0))
                ranks[gi] = ranks[gi] + hit
        rank = jnp.concatenate(ranks, axis=0)
        keep = ((blk < i) & (rank < MOBA_TOPK)) | (blk == i)
        bias_t = jnp.concatenate([jnp.where(keep, 0.0, NEG_INF), pad], axis=0)
        qa_ref[h, :, :MOBA_DH] = (q * Q_SCALE).astype(BF16)
        qa_ref[h, :, MOBA_DH:] = bias_t.T.astype(BF16)
        ka_ref[h, :, :MOBA_DH] = k.astype(BF16)
        ka_ref[h, :, MOBA_DH:] = onehot
        v = v_ref[:, cols]
        vt_ref[h] = v.T.astype(BF16)
        kr_ref[:, h, :] = k
        vr_ref[:, h, :] = v
        km_scr[h, pl.ds(i, 1), :] = jnp.mean(k, axis=0, keepdims=True)


def _moba_prep(z_p):
    col = lambda base: (lambda i: (i, base // HD))
    out_idx = lambda i: (0, i, 0)
    return pl.pallas_call(
        _moba_prep_kernel,
        grid=(N_BLK,),
        in_specs=[pl.BlockSpec((MOBA_BLOCK, HD), col(COL_QB)),
                  pl.BlockSpec((MOBA_BLOCK, HD), col(COL_KB)),
                  pl.BlockSpec((MOBA_BLOCK, HD), col(COL_VB))],
        out_specs=[pl.BlockSpec((HEADS, MOBA_BLOCK, AUG_W), out_idx),
                   pl.BlockSpec((HEADS, MOBA_BLOCK, AUG_W), out_idx),
                   pl.BlockSpec((HEADS, None, MOBA_DH, MOBA_BLOCK), lambda i: (0, i // FL_KC, 0, i % FL_KC)),
                   pl.BlockSpec((MOBA_BLOCK, HEADS, MOBA_DH), lambda i: (i, 0, 0)),
                   pl.BlockSpec((MOBA_BLOCK, HEADS, MOBA_DH), lambda i: (i, 0, 0))],
        out_shape=[jax.ShapeDtypeStruct((HEADS, SEQ, AUG_W), BF16),
                   jax.ShapeDtypeStruct((HEADS, SEQ, AUG_W), BF16),
                   jax.ShapeDtypeStruct((HEADS, N_CHUNK, MOBA_DH, FL_CH), BF16),
                   jax.ShapeDtypeStruct((SEQ, HEADS, MOBA_DH), F32),
                   jax.ShapeDtypeStruct((SEQ, HEADS, MOBA_DH), F32)],
        scratch_shapes=[pltpu.VMEM((HEADS, N_BLK, MOBA_DH), F32)],
        compiler_params=_params(1),
        name="moba_prep",
    )(z_p, z_p, z_p)


FL_HEADS = 2


def _moba_flash_kernel(q_ref, k_ref, vt_ref, g_ref, o_ref, sa_ref, sb_ref):
    i = pl.program_id(1)
    last = i // FL_KC
    heads = range(FL_HEADS)

    def logits(hh, c):
        rows = pl.ds(pl.multiple_of(c * FL_CH, FL_CH), FL_CH)
        return lax.dot_general(k_ref[hh, rows, :], q_ref[hh], NT_DIMS, preferred_element_type=F32)

    def update(carry, s_t, vt):
        m, l, acc = carry
        m_new = jnp.maximum(m, jnp.max(s_t, axis=0, keepdims=True))
        alpha = jnp.exp2(m - m_new)
        p = jnp.exp2(s_t - m_new)
        l = alpha * l + jnp.sum(p, axis=0, keepdims=True)
        acc = alpha * acc + jnp.dot(vt, p.astype(BF16), preferred_element_type=F32)
        return m_new, l, acc

    kpos = last * FL_CH + lax.broadcasted_iota(jnp.int32, (FL_CH, MOBA_BLOCK), 0)
    qpos = i * MOBA_BLOCK + lax.broadcasted_iota(jnp.int32, (FL_CH, MOBA_BLOCK), 1)
    causal = kpos <= qpos
    for hh in heads:
        sa_ref[hh] = logits(hh, 0)
    carries = []
    for hh in heads:
        s_t = jnp.where(causal, logits(hh, last), NEG_INF)
        m = jnp.max(s_t, axis=0, keepdims=True)
        p = jnp.exp2(s_t - m)
        l = jnp.sum(p, axis=0, keepdims=True)
        acc = jnp.dot(vt_ref[hh, last], p.astype(BF16), preferred_element_type=F32)
        carries.append((m, l, acc))

    def consume(carries, s_ref, c):
        return tuple(update(carries[hh], s_ref[hh], vt_ref[hh, c]) for hh in heads)

    def pair(t, carries):
        c = 2 * t
        for hh in heads:
            sb_ref[hh] = logits(hh, c + 1)
        carries = consume(carries, sa_ref, c)
        for hh in heads:
            sa_ref[hh] = logits(hh, c + 2)
        return consume(carries, sb_ref, c + 1)

    carries = lax.fori_loop(0, last // 2, pair, tuple(carries))
    carries = lax.cond(last % 2 == 1, lambda cs: consume(cs, sa_ref, last - 1), lambda cs: cs, carries)
    for hh in heads:
        m, l, acc = carries[hh]
        cols = slice(hh * MOBA_DH, (hh + 1) * MOBA_DH)
        g = g_ref[:, cols]
        o_ref[:, cols] = ((g * jax.nn.sigmoid(g)) * (acc / l).T).astype(o_ref.dtype)


PAST_BLOCKS = PAST_LEN // MOBA_BLOCK


PAGE_ROWS = PAGE_SIZE * HEADS
PAGES_PER_BLOCK = MOBA_BLOCK // PAGE_SIZE


def _sample_page_copies(pt_ref, seq, slot, ck_hbm, cv_hbm, kbuf, vbuf, sem):
    copies = []
    for p in range(N_PAGES):
        page = pt_ref[seq, p]
        copies.append(pltpu.make_async_copy(ck_hbm.at[page], kbuf.at[slot, p], sem.at[0, slot]))
        copies.append(pltpu.make_async_copy(cv_hbm.at[page], vbuf.at[slot, p], sem.at[1, slot]))
    return copies


def _moba_sample_step(step, n_steps, pt_ref, x_ref, ck_hbm, cv_hbm, o_ref, kbuf, vbuf, sem):
    slot = step % 2
    copies = functools.partial(_sample_page_copies, pt_ref, ck_hbm=ck_hbm, cv_hbm=cv_hbm,
                               kbuf=kbuf, vbuf=vbuf, sem=sem)

    @pl.when(step == 0)
    def _():
        for cp in copies(0, 0):
            cp.start()

    @pl.when(step + 1 < n_steps)
    def _():
        for cp in copies(step + 1, 1 - slot):
            cp.start()

    for cp in copies(step, slot):
        cp.wait()

    q = x_ref[0:HEADS, :]
    k_new = x_ref[HEADS:2 * HEADS, :]
    v_new = x_ref[2 * HEADS:3 * HEADS, :]
    g = x_ref[3 * HEADS:4 * HEADS, :]
    qs = q * (MOBA_DH ** -0.5)
    q16 = qs.astype(BF16)

    score = []
    for j in range(PAST_BLOCKS):
        ksum = sum(jnp.sum(kbuf[slot, PAGES_PER_BLOCK * j + r].reshape(PAGE_SIZE, HEADS, MOBA_DH), axis=0)
                   for r in range(PAGES_PER_BLOCK))
        score.append(jnp.sum(q * ksum, axis=-1, keepdims=True))
    keep = []
    for j in range(PAST_BLOCKS):
        rank = jnp.zeros((HEADS, 1), F32)
        for jp in range(PAST_BLOCKS):
            if jp != j:
                beats = (score[jp] >= score[j]) if jp < j else (score[jp] > score[j])
                rank = rank + jnp.where(beats, 1.0, 0.0)
        keep.append(jnp.where(rank < MOBA_TOPK, 1.0, 0.0))
    packed = jnp.zeros((HEADS, 1), F32)
    before = jnp.zeros((HEADS, 1), F32)
    for j in range(PAST_BLOCKS):
        for t in range(MOBA_TOPK):
            packed = packed + jnp.where(before == t, keep[j] * float(j * PAST_BLOCKS ** t), 0.0)
        before = before + keep[j]
    packed = packed.astype(jnp.int32)

    def head_rows(buf, h, t):
        blk = (packed[h, 0] // (PAST_BLOCKS ** t)) % PAST_BLOCKS
        return jnp.concatenate(
            [buf[slot, PAGES_PER_BLOCK * blk + r, pl.ds(h, PAGE_SIZE, stride=HEADS), :]
             for r in range(PAGES_PER_BLOCK)], axis=0).astype(BF16)

    logits = []
    for t in range(MOBA_TOPK):
        rows = []
        for h in range(HEADS):
            s = lax.dot_general(q16, head_rows(kbuf, h, t), NT_DIMS, preferred_element_type=F32)
            rows.append(s[h:h + 1, :])
        logits.append(jnp.concatenate(rows, axis=0))
    s_self = jnp.sum(qs * k_new, axis=-1, keepdims=True)
    m = functools.reduce(jnp.maximum, [jnp.max(s, axis=-1, keepdims=True) for s in logits] + [s_self])
    probs = [jnp.exp(s - m) for s in logits]
    p_self = jnp.exp(s_self - m)
    l = functools.reduce(jnp.add, [jnp.sum(p, axis=-1, keepdims=True) for p in probs] + [p_self])
    probs16 = [p.astype(BF16) for p in probs]
    y_rows = []
    for h in range(HEADS):
        acc = jnp.zeros((1, MOBA_DH), F32)
        for t in range(MOBA_TOPK):
            r = jnp.dot(probs16[t], head_rows(vbuf, h, t), preferred_element_type=F32)
            acc = acc + r[h:h + 1, :]
        y_rows.append(acc)
    y = (jnp.concatenate(y_rows, axis=0) + p_self * v_new) / l
    o_ref[...] = (g * jax.nn.sigmoid(g)) * y


FL_STEPS_PER_GROUP = N_BLK
assert (HEADS // FL_HEADS) * FL_STEPS_PER_GROUP == DEC_BATCH


def _moba_fused_kernel(pt_ref, q_ref, k_ref, vt_ref, g_ref, x_ref, ck_hbm, cv_hbm, o_ref, os_ref,
                       sa_ref, sb_ref, kbuf, vbuf, sem):
    step = pl.program_id(0) * FL_STEPS_PER_GROUP + pl.program_id(1)
    _moba_sample_step(step, DEC_BATCH, pt_ref, x_ref, ck_hbm, cv_hbm, os_ref, kbuf, vbuf, sem)
    _moba_flash_kernel(q_ref, k_ref, vt_ref, g_ref, o_ref, sa_ref, sb_ref)


def _moba_fused(page_table, q_aug, k_aug, v_t, z_p, x_s, cache_k, cache_v):
    gw = FL_HEADS * MOBA_DH
    seq = lambda h, i: h * FL_STEPS_PER_GROUP + i
    resident = pl.Buffered(1)
    grid_spec = pltpu.PrefetchScalarGridSpec(
        num_scalar_prefetch=1,
        grid=(HEADS // FL_HEADS, N_BLK),
        in_specs=([pl.BlockSpec((FL_HEADS, MOBA_BLOCK, AUG_W), lambda h, i, pt: (h, i, 0)),
                   pl.BlockSpec((FL_HEADS, SEQ, AUG_W), lambda h, i, pt: (h, 0, 0), pipeline_mode=resident),
                   pl.BlockSpec((FL_HEADS, N_CHUNK, MOBA_DH, FL_CH), lambda h, i, pt: (h, 0, 0, 0),
                                pipeline_mode=resident),
                   pl.BlockSpec((MOBA_BLOCK, gw), lambda h, i, pt: (i, COL_GB // gw + h)),
                   pl.BlockSpec((None, 4 * HEADS, MOBA_DH), lambda h, i, pt: (seq(h, i), 0, 0)),
                   pl.BlockSpec(memory_space=pl.ANY),
                   pl.BlockSpec(memory_space=pl.ANY)]),
        out_specs=[pl.BlockSpec((MOBA_BLOCK, gw), lambda h, i, pt: (i, h)),
                   pl.BlockSpec((None, HEADS, MOBA_DH), lambda h, i, pt: (seq(h, i), 0, 0))],
        scratch_shapes=[pltpu.VMEM((FL_HEADS, FL_CH, MOBA_BLOCK), F32),
                        pltpu.VMEM((FL_HEADS, FL_CH, MOBA_BLOCK), F32),
                        pltpu.VMEM((2, N_PAGES, PAGE_ROWS, MOBA_DH), F32),
                        pltpu.VMEM((2, N_PAGES, PAGE_ROWS, MOBA_DH), F32),
                        pltpu.SemaphoreType.DMA((2, 2))],
    )
    return pl.pallas_call(
        _moba_fused_kernel,
        grid_spec=grid_spec,
        out_shape=[jax.ShapeDtypeStruct((SEQ, HEADS * MOBA_DH), BF16),
                   jax.ShapeDtypeStruct((DEC_BATCH, HEADS, MOBA_DH), F32)],
        compiler_params=_params(2),
        name="moba_fused",
    )(page_table, q_aug, k_aug, v_t, z_p, x_s, cache_k, cache_v)


def _out_proj_kernel(ga_ref, gb_ref, zma_ref, zmb_ref, x_ref, bm_ref, wa_ref, wb_ref, wo_ref, gp_ref, y_ref):
    u_a = jnp.dot(ga_ref[...], wa_ref[...], preferred_element_type=F32)
    u_b = jnp.dot(gb_ref[...], wb_ref[...], preferred_element_type=F32)
    gate_a = jax.nn.sigmoid(zma_ref[...] + bm_ref[0:1, :])
    gate_b = jax.nn.sigmoid(zmb_ref[...] + bm_ref[1:2, :])
    merged = (gate_a * u_a + gate_b * u_b).astype(BF16)
    o = jnp.dot(merged, wo_ref[...], preferred_element_type=F32)
    on = o * lax.rsqrt(jnp.mean(o * o, axis=-1, keepdims=True) + NORM_EPS)
    y_ref[...] = x_ref[...] + on * gp_ref[...]


def _out_proj(ga, gb, z, x, b_merge, w_a, w_b, w_o, g_post, tile):
    rows = x.shape[0]
    resident = lambda shape: pl.BlockSpec(shape, lambda i: (0, 0), pipeline_mode=pl.Buffered(1))
    return pl.pallas_call(
        _out_proj_kernel,
        grid=(rows // tile,),
        in_specs=[pl.BlockSpec((tile, HEADS * RET_DV), lambda i: (i, 0)),
                  pl.BlockSpec((tile, HD), lambda i: (i, 0)),
                  pl.BlockSpec((tile, D_MODEL), lambda i: (i, COL_ZM // D_MODEL)),
                  pl.BlockSpec((tile, D_MODEL), lambda i: (i, COL_ZM // D_MODEL + 1)),
                  pl.BlockSpec((tile, D_MODEL), lambda i: (i, 0)),
                  resident((2, D_MODEL)),
                  resident((HEADS * RET_DV, D_MODEL)),
                  resident((HD, D_MODEL)),
                  resident((D_MODEL, D_MODEL)),
                  resident((1, D_MODEL))],
        out_specs=pl.BlockSpec((tile, D_MODEL), lambda i: (i, 0)),
        out_shape=jax.ShapeDtypeStruct((rows, D_MODEL), F32),
        compiler_params=_params(1),
        name="out_proj",
    )(ga, gb, z, z, x, b_merge, w_a, w_b, w_o, g_post)


def _rope_tables(pos):
    half = RET_DK // 2
    inv = 1.0 / (RET_ROPE_BASE ** jnp.linspace(0.0, 1.0, half, dtype=F32))
    ang = pos.astype(F32)[:, None] * inv[None, :]
    cos, sin = jnp.cos(ang), jnp.sin(ang)
    return jnp.concatenate([cos, cos], axis=-1), jnp.concatenate([-sin, sin], axis=-1)


def kernel(x_prompt, x_sample, state_ret, cache_k, cache_v, page_table, norm_pre, w_in,
           w_a_out, w_b_out, b_merge, w_out, norm_post):
    xp = x_prompt.reshape(SEQ, D_MODEL)
    xs = x_sample.reshape(DEC_BATCH, D_MODEL)
    g_pre = norm_pre.reshape(1, D_MODEL)
    g_post = norm_post.reshape(1, D_MODEL)

    log_g = jnp.log(1.0 - 2.0 ** (-5.0 - jnp.arange(HEADS, dtype=F32)))
    dec = jnp.stack([log_g, jnp.exp(log_g * RET_CHUNK), jnp.exp(log_g * 1.0)])
    cos_p, sin_p = _rope_tables(jnp.arange(SEQ, dtype=jnp.int32))
    cos_s, sin_s = _rope_tables(PAST_LEN + jnp.arange(1, dtype=jnp.int32))

    hn_p = _rmsnorm(xp, g_pre, 1024)
    hn_s = _rmsnorm(xs, g_pre, DEC_BATCH)
    z_p, z_s = _in_proj(hn_p, hn_s, w_in.reshape(D_MODEL, PROJ_W))

    ga_p, st_p = _ret_prompt(dec, z_p, cos_p, sin_p)
    ga_s, st_s = _ret_sample(dec, z_s, cos_s, sin_s,
                             state_ret.reshape(DEC_BATCH, HEADS, RET_DK, RET_DV))

    q_aug, k_aug, v_t, k_rows_p, v_rows_p = _moba_prep(z_p)
    n_phys = cache_k.shape[1]
    gb_p, gb_s = _moba_fused(page_table, q_aug, k_aug, v_t, z_p,
                             z_s[:, COL_QB:COL_ZM].reshape(DEC_BATCH, 4 * HEADS, MOBA_DH),
                             cache_k.reshape(n_phys, PAGE_ROWS, MOBA_DH),
                             cache_v.reshape(n_phys, PAGE_ROWS, MOBA_DH))

    w_a = w_a_out.reshape(HEADS * RET_DV, D_MODEL).astype(BF16)
    w_b = w_b_out.reshape(HD, D_MODEL).astype(BF16)
    w_o = w_out.reshape(D_MODEL, D_MODEL).astype(BF16)
    bm = b_merge.reshape(2, D_MODEL)
    y_p = _out_proj(ga_p, gb_p, z_p, xp, bm, w_a, w_b, w_o, g_post, 256)
    y_s = _out_proj(ga_s.astype(BF16), gb_s.reshape(DEC_BATCH, HD).astype(BF16), z_s, xs, bm,
                    w_a, w_b, w_o, g_post, DEC_BATCH)

    return (y_p.reshape(1, SEQ, D_MODEL),
            y_s.reshape(DEC_BATCH, 1, D_MODEL),
            st_p.reshape(1, 1, HEADS, RET_DK, RET_DV),
            st_s.reshape(1, DEC_BATCH, HEADS, RET_DK, RET_DV),
            k_rows_p.reshape(1, 1, SEQ, HEADS, MOBA_DH),
            v_rows_p.reshape(1, 1, SEQ, HEADS, MOBA_DH),
            z_s[:, COL_KB:COL_VB].reshape(1, DEC_BATCH, 1, HEADS, MOBA_DH),
            z_s[:, COL_VB:COL_GB].reshape(1, DEC_BATCH, 1, HEADS, MOBA_DH))
```

```python
import functools

import jax
import jax.numpy as jnp
from jax import lax
from jax.experimental import pallas as pl
from jax.experimental.pallas import tpu as pltpu

F32 = jnp.float32
BF16 = jnp.bfloat16

D_MODEL = 2048
SEQ = 8192
DEC_BATCH = 128
PAST_LEN = 2048
PAGE_SIZE = 128
N_PAGES = PAST_LEN // PAGE_SIZE
HEADS = 8
RET_DK = 128
RET_DV = 256
RET_CHUNK = 128
RET_ROPE_BASE = 10000.0
MOBA_DH = 128
HD = HEADS * MOBA_DH
MOBA_BLOCK = 256
MOBA_TOPK = 3
NORM_EPS = 1e-6
NEG_INF = -1e30

COL_QA, COL_KA, COL_VA, COL_GA = 0, 1024, 2048, 4096
COL_QB, COL_KB, COL_VB, COL_GB, COL_ZM = 6144, 7168, 8192, 9216, 10240
PROJ_W = 14336

V7X_VMEM_LIMIT = 56 * 1024 * 1024

NT_DIMS = (((1,), (1,)), ((), ()))
TN_DIMS = (((0,), (0,)), ((), ()))


def _params(n_axes, vmem=V7X_VMEM_LIMIT):
    return pltpu.CompilerParams(dimension_semantics=("arbitrary",) * n_axes, vmem_limit_bytes=vmem)


def _rmsnorm_kernel(x_ref, g_ref, o_ref):
    x = x_ref[...]
    y = x * lax.rsqrt(jnp.mean(x * x, axis=-1, keepdims=True) + NORM_EPS)
    o_ref[...] = (y * g_ref[...]).astype(o_ref.dtype)


def _rmsnorm(x, gain, tile):
    rows = x.shape[0]
    return pl.pallas_call(
        _rmsnorm_kernel,
        grid=(rows // tile,),
        in_specs=[pl.BlockSpec((tile, D_MODEL), lambda i: (i, 0)),
                  pl.BlockSpec((1, D_MODEL), lambda i: (0, 0))],
        out_specs=pl.BlockSpec((tile, D_MODEL), lambda i: (i, 0)),
        out_shape=jax.ShapeDtypeStruct((rows, D_MODEL), BF16),
        compiler_params=_params(1),
        name="rmsnorm_pre",
    )(x, gain)


IN_TM = 1024
IN_TN = 1024
IN_MP = SEQ // IN_TM


def _in_proj_kernel(hp_ref, hs_ref, w_ref, zp_ref, zs_ref, wb_ref):
    m = pl.program_id(1)

    @pl.when(m == 0)
    def _():
        wb_ref[...] = w_ref[...].astype(BF16)

    @pl.when(m < IN_MP)
    def _():
        zp_ref[...] = jnp.dot(hp_ref[...], wb_ref[...], preferred_element_type=F32)

    @pl.when(m == IN_MP)
    def _():
        zs_ref[...] = jnp.dot(hs_ref[...], wb_ref[...], preferred_element_type=F32)


def _in_proj(hn_p, hn_s, w_in):
    last = IN_MP - 1
    return pl.pallas_call(
        _in_proj_kernel,
        grid=(PROJ_W // IN_TN, IN_MP + 1),
        in_specs=[pl.BlockSpec((IN_TM, D_MODEL), lambda n, m: (jnp.minimum(m, last), 0)),
                  pl.BlockSpec((DEC_BATCH, D_MODEL), lambda n, m: (0, 0)),
                  pl.BlockSpec((D_MODEL, IN_TN), lambda n, m: (0, n))],
        out_specs=[pl.BlockSpec((IN_TM, IN_TN), lambda n, m: (jnp.minimum(m, last), n)),
                   pl.BlockSpec((DEC_BATCH, IN_TN), lambda n, m: (0, n))],
        out_shape=[jax.ShapeDtypeStruct((SEQ, PROJ_W), F32),
                   jax.ShapeDtypeStruct((DEC_BATCH, PROJ_W), F32)],
        scratch_shapes=[pltpu.VMEM((D_MODEL, IN_TN), BF16)],
        compiler_params=_params(2),
        name="in_proj",
    )(hn_p, hn_s, w_in)


def _rotate(x, cos2, sin2):
    return x * cos2 + pltpu.roll(x, RET_DK // 2, 1) * sin2


def _group_norm_gate(y, g):
    yn = y * lax.rsqrt(jnp.mean(y * y, axis=-1, keepdims=True) + NORM_EPS)
    return (g * jax.nn.sigmoid(g)) * yn


def _ret_prompt_kernel(dec_ref, q_ref, k_ref, v_ref, g_ref, cos_r_ref, sin_r_ref, cos_c_ref, sin_c_ref,
                       ga_ref, st_ref, s_scr, din_scr, dq_scr, dk_scr):
    c = pl.program_id(0)
    C = RET_CHUNK

    @pl.when(c == 0)
    def _():
        ii = lax.broadcasted_iota(jnp.int32, (C, C), 0).astype(F32)
        jj = lax.broadcasted_iota(jnp.int32, (C, C), 1).astype(F32)
        diff = ii - jj
        for h in range(HEADS):
            lg = dec_ref[0, h]
            din_scr[h] = jnp.where(diff >= 0, jnp.exp(lg * jnp.maximum(diff, 0.0)), 0.0)
            dq_scr[h] = jnp.exp(lg * (ii + 1.0))
            dk_scr[h] = jnp.exp(lg * (C - 1.0 - ii))
        s_scr[...] = jnp.zeros_like(s_scr)

    cos_r, sin_r = cos_r_ref[...], sin_r_ref[...]
    cos_c, sin_c = cos_c_ref[...], sin_c_ref[...]
    cos2 = cos_c * cos_r - sin_c * sin_r
    sin2 = sin_c * cos_r + cos_c * sin_r
    for h in range(HEADS):
        qr = _rotate(q_ref[:, h * RET_DK:(h + 1) * RET_DK], cos2, sin2)
        kr = _rotate(k_ref[:, h * RET_DK:(h + 1) * RET_DK], cos2, sin2) * (RET_DK ** -0.5)
        vb = v_ref[:, h * RET_DV:(h + 1) * RET_DV].astype(BF16)
        att = lax.dot_general(qr.astype(BF16), kr.astype(BF16), NT_DIMS, preferred_element_type=F32)
        inner = jnp.dot((att * din_scr[h]).astype(BF16), vb, preferred_element_type=F32)
        s_old = s_scr[h]
        cross = jnp.dot((qr * dq_scr[h]).astype(BF16), s_old.astype(BF16), preferred_element_type=F32)
        kv = lax.dot_general((kr * dk_scr[h]).astype(BF16), vb, TN_DIMS, preferred_element_type=F32)
        s_scr[h] = dec_ref[1, h] * s_old + kv
        gated = _group_norm_gate(inner + cross, g_ref[:, h * RET_DV:(h + 1) * RET_DV])
        ga_ref[:, h * RET_DV:(h + 1) * RET_DV] = gated.astype(ga_ref.dtype)

    @pl.when(c == pl.num_programs(0) - 1)
    def _():
        st_ref[...] = s_scr[...]


def _ret_prompt(dec, z_p, rope_row, rope_chunk):
    C = RET_CHUNK
    cos_r, sin_r = rope_row
    cos_c, sin_c = rope_chunk
    grid_spec = pltpu.PrefetchScalarGridSpec(
        num_scalar_prefetch=1,
        grid=(SEQ // C,),
        in_specs=[pl.BlockSpec((C, HEADS * RET_DK), lambda c, d: (c, COL_QA // 1024)),
                  pl.BlockSpec((C, HEADS * RET_DK), lambda c, d: (c, COL_KA // 1024)),
                  pl.BlockSpec((C, HEADS * RET_DV), lambda c, d: (c, COL_VA // 2048)),
                  pl.BlockSpec((C, HEADS * RET_DV), lambda c, d: (c, COL_GA // 2048)),
                  pl.BlockSpec((C, RET_DK), lambda c, d: (0, 0)),
                  pl.BlockSpec((C, RET_DK), lambda c, d: (0, 0)),
                  pl.BlockSpec((None, 1, RET_DK), lambda c, d: (c, 0, 0)),
                  pl.BlockSpec((None, 1, RET_DK), lambda c, d: (c, 0, 0))],
        out_specs=[pl.BlockSpec((C, HEADS * RET_DV), lambda c, d: (c, 0)),
                   pl.BlockSpec((HEADS, RET_DK, RET_DV), lambda c, d: (0, 0, 0))],
        scratch_shapes=[pltpu.VMEM((HEADS, RET_DK, RET_DV), F32),
                        pltpu.VMEM((HEADS, C, C), F32),
                        pltpu.VMEM((HEADS, C, C), F32),
                        pltpu.VMEM((HEADS, C, C), F32)],
    )
    return pl.pallas_call(
        _ret_prompt_kernel,
        grid_spec=grid_spec,
        out_shape=[jax.ShapeDtypeStruct((SEQ, HEADS * RET_DV), BF16),
                   jax.ShapeDtypeStruct((HEADS, RET_DK, RET_DV), F32)],
        compiler_params=_params(1),
        name="retention_prompt",
    )(dec, z_p, z_p, z_p, z_p, cos_r, sin_r, cos_c, sin_c)


RS_TB = 8


def _ret_sample_kernel(dec_ref, q_ref, k_ref, v_ref, g_ref, cos_ref, sin_ref, s0_ref, ga_ref, s1_ref, y_scr):
    i = pl.program_id(0)
    rows = pl.ds(pl.multiple_of(i * RS_TB, RS_TB), RS_TB)
    cos2 = cos_ref[...]
    sin2 = sin_ref[...]
    eye = (lax.broadcasted_iota(jnp.int32, (RET_DK, RET_DK), 0)
           == lax.broadcasted_iota(jnp.int32, (RET_DK, RET_DK), 1))
    ones = jnp.ones((RET_DK, RET_DV), BF16)

    def lane_bcast_column(row):
        diag = jnp.where(eye, jnp.broadcast_to(row, (RET_DK, RET_DK)), 0.0).astype(BF16)
        return jnp.dot(diag, ones, preferred_element_type=F32)

    for h in range(HEADS):
        gamma = dec_ref[2, h]
        qr = _rotate(q_ref[rows, h * RET_DK:(h + 1) * RET_DK], cos2, sin2)
        kr = _rotate(k_ref[rows, h * RET_DK:(h + 1) * RET_DK], cos2, sin2) * (RET_DK ** -0.5)
        vh = v_ref[rows, h * RET_DV:(h + 1) * RET_DV]
        qk = jnp.sum(qr * kr, axis=-1, keepdims=True)
        inner = qk * vh
        for b in range(RS_TB):
            s_old = s0_ref[b, h]
            cross = jnp.sum(lane_bcast_column(qr[b:b + 1, :]) * s_old, axis=0, keepdims=True)
            s1_ref[b, h] = gamma * s_old + lane_bcast_column(kr[b:b + 1, :]) * vh[b:b + 1, :]
            y_scr[b:b + 1, h * RET_DV:(h + 1) * RET_DV] = inner[b:b + 1, :] + cross * gamma
    for h in range(HEADS):
        cols = slice(h * RET_DV, (h + 1) * RET_DV)
        ga_ref[:, cols] = _group_norm_gate(y_scr[:, cols], g_ref[rows, cols])


def _ret_sample(dec, z_s, cos2, sin2, state):
    full = lambda shape, col: pl.BlockSpec(shape, lambda i, d: (0, col))
    grid_spec = pltpu.PrefetchScalarGridSpec(
        num_scalar_prefetch=1,
        grid=(DEC_BATCH // RS_TB,),
        in_specs=[full((DEC_BATCH, HEADS * RET_DK), COL_QA // 1024),
                  full((DEC_BATCH, HEADS * RET_DK), COL_KA // 1024),
                  full((DEC_BATCH, HEADS * RET_DV), COL_VA // 2048),
                  full((DEC_BATCH, HEADS * RET_DV), COL_GA // 2048),
                  full((1, RET_DK), 0),
                  full((1, RET_DK), 0),
                  pl.BlockSpec((RS_TB, HEADS, RET_DK, RET_DV), lambda i, d: (i, 0, 0, 0))],
        out_specs=[pl.BlockSpec((RS_TB, HEADS * RET_DV), lambda i, d: (i, 0)),
                   pl.BlockSpec((RS_TB, HEADS, RET_DK, RET_DV), lambda i, d: (i, 0, 0, 0))],
        scratch_shapes=[pltpu.VMEM((RS_TB, HEADS * RET_DV), F32)],
    )
    return pl.pallas_call(
        _ret_sample_kernel,
        grid_spec=grid_spec,
        out_shape=[jax.ShapeDtypeStruct((DEC_BATCH, HEADS * RET_DV), F32),
                   jax.ShapeDtypeStruct((DEC_BATCH, HEADS, RET_DK, RET_DV), F32)],
        compiler_params=_params(1),
        name="retention_sample",
    )(dec, z_s, z_s, z_s, z_s, cos2, sin2, state)


N_BLK = SEQ // MOBA_BLOCK
BIAS_W = 128
AUG_W = MOBA_DH + BIAS_W


FL_KC = 4
FL_CH = FL_KC * MOBA_BLOCK
N_CHUNK = SEQ // FL_CH
LOG2E = 1.4426950408889634
Q_SCALE = (MOBA_DH ** -0.5) * LOG2E


def _moba_prep_kernel(q_ref, k_ref, v_ref, qa_ref, ka_ref, vt_ref, kr_ref, vr_ref, km_scr):
    i = pl.program_id(0)

    @pl.when(i == 0)
    def _():
        km_scr[...] = jnp.zeros_like(km_scr)

    blk = lax.broadcasted_iota(jnp.int32, (N_BLK, MOBA_BLOCK), 0)
    sub8 = lax.broadcasted_iota(jnp.int32, (8, MOBA_BLOCK), 0)
    lane = lax.broadcasted_iota(jnp.int32, (MOBA_BLOCK, BIAS_W), 1)
    onehot = jnp.where(lane == i, 1.0, 0.0).astype(BF16)
    pad = jnp.full((BIAS_W - N_BLK, MOBA_BLOCK), NEG_INF, F32)
    for h in range(HEADS):
        cols = slice(h * MOBA_DH, (h + 1) * MOBA_DH)
        q = q_ref[:, cols]
        k = k_ref[:, cols]
        s_t = lax.dot_general(km_scr[h], q, NT_DIMS, precision=lax.Precision.HIGHEST,
                              preferred_element_type=F32)
        groups = [s_t[r:r + 8, :] for r in range(0, N_BLK, 8)]
        ranks = [jnp.zeros((8, MOBA_BLOCK), F32) for _ in groups]
        for jp in range(N_BLK):
            sj = s_t[jp:jp + 1, :]
            is_past = jnp.where(jp < i, 1.0, 0.0)
            g0, r0 = divmod(jp, 8)
            for gi, grp in enumerate(groups):
                if gi < g0:
                    hit = jnp.where(sj > grp, is_past, 0.0)
                elif gi > g0:
                    hit = jnp.where(sj >= grp, is_past, 0.0)
                else:
                    hit = jnp.where(sub8 > r0, jnp.where(sj >= grp, is_past, 0.0),
                                    jnp.where(sj > grp, is_past, 0.0))
                ranks[gi] = ranks[gi] + hit
        rank = jnp.concatenate(ranks, axis=0)
        keep = ((blk < i) & (rank < MOBA_TOPK)) | (blk == i)
        bias_t = jnp.concatenate([jnp.where(keep, 0.0, NEG_INF), pad], axis=0)
        qa_ref[h, :, :MOBA_DH] = (q * Q_SCALE).astype(BF16)
        qa_ref[h, :, MOBA_DH:] = bias_t.T.astype(BF16)
        ka_ref[h, :, :MOBA_DH] = k.astype(BF16)
        ka_ref[h, :, MOBA_DH:] = onehot
        v = v_ref[:, cols]
        vt_ref[h] = v.T.astype(BF16)
        kr_ref[:, h, :] = k
        vr_ref[:, h, :] = v
        km_scr[h, pl.ds(i, 1), :] = jnp.mean(k, axis=0, keepdims=True)


def _moba_prep(z_p):
    col = lambda base: (lambda i: (i, base // HD))
    out_idx = lambda i: (0, i, 0)
    return pl.pallas_call(
        _moba_prep_kernel,
        grid=(N_BLK,),
        in_specs=[pl.BlockSpec((MOBA_BLOCK, HD), col(COL_QB)),
                  pl.BlockSpec((MOBA_BLOCK, HD), col(COL_KB)),
                  pl.BlockSpec((MOBA_BLOCK, HD), col(COL_VB))],
        out_specs=[pl.BlockSpec((HEADS, MOBA_BLOCK, AUG_W), out_idx),
                   pl.BlockSpec((HEADS, MOBA_BLOCK, AUG_W), out_idx),
                   pl.BlockSpec((HEADS, None, MOBA_DH, MOBA_BLOCK), lambda i: (0, i // FL_KC, 0, i % FL_KC)),
                   pl.BlockSpec((MOBA_BLOCK, HEADS, MOBA_DH), lambda i: (i, 0, 0)),
                   pl.BlockSpec((MOBA_BLOCK, HEADS, MOBA_DH), lambda i: (i, 0, 0))],
        out_shape=[jax.ShapeDtypeStruct((HEADS, SEQ, AUG_W), BF16),
                   jax.ShapeDtypeStruct((HEADS, SEQ, AUG_W), BF16),
                   jax.ShapeDtypeStruct((HEADS, N_CHUNK, MOBA_DH, FL_CH), BF16),
                   jax.ShapeDtypeStruct((SEQ, HEADS, MOBA_DH), F32),
                   jax.ShapeDtypeStruct((SEQ, HEADS, MOBA_DH), F32)],
        scratch_shapes=[pltpu.VMEM((HEADS, N_BLK, MOBA_DH), F32)],
        compiler_params=_params(1),
        name="moba_prep",
    )(z_p, z_p, z_p)


FL_HEADS = 2


def _moba_flash_kernel(q_ref, k_ref, vt_ref, g_ref, o_ref, sa_ref, sb_ref):
    i = pl.program_id(1)
    last = i // FL_KC
    heads = range(FL_HEADS)

    def logits(hh, c):
        rows = pl.ds(pl.multiple_of(c * FL_CH, FL_CH), FL_CH)
        return lax.dot_general(k_ref[hh, rows, :], q_ref[hh], NT_DIMS, preferred_element_type=F32)

    def update(carry, s_t, vt):
        m, l, acc = carry
        m_new = jnp.maximum(m, jnp.max(s_t, axis=0, keepdims=True))
        alpha = jnp.exp2(m - m_new)
        p = jnp.exp2(s_t - m_new)
        l = alpha * l + jnp.sum(p, axis=0, keepdims=True)
        acc = alpha * acc + jnp.dot(vt, p.astype(BF16), preferred_element_type=F32)
        return m_new, l, acc

    for hh in heads:
        sa_ref[hh] = logits(hh, 0)

    def own_chunk(n_blocks):
        n = n_blocks * MOBA_BLOCK

        def run():
            rows = pl.ds(pl.multiple_of(last * FL_CH, FL_CH), n)
            kpos = last * FL_CH + lax.broadcasted_iota(jnp.int32, (n, MOBA_BLOCK), 0)
            qpos = i * MOBA_BLOCK + lax.broadcasted_iota(jnp.int32, (n, MOBA_BLOCK), 1)
            out = []
            for hh in heads:
                s_t = lax.dot_general(k_ref[hh, rows, :], q_ref[hh], NT_DIMS, preferred_element_type=F32)
                s_t = jnp.where(kpos <= qpos, s_t, NEG_INF)
                m = jnp.max(s_t, axis=0, keepdims=True)
                p = jnp.exp2(s_t - m)
                l = jnp.sum(p, axis=0, keepdims=True)
                acc = jnp.dot(vt_ref[hh, last, :, 0:n], p.astype(BF16), preferred_element_type=F32)
                out.append((m, l, acc))
            return tuple(out)
        return run

    carries = lax.switch(i % FL_KC, [own_chunk(b + 1) for b in range(FL_KC)])

    def consume(carries, s_ref, c):
        return tuple(update(carries[hh], s_ref[hh], vt_ref[hh, c]) for hh in heads)

    def pair(t, carries):
        c = 2 * t
        for hh in heads:
            sb_ref[hh] = logits(hh, c + 1)
        carries = consume(carries, sa_ref, c)
        for hh in heads:
            sa_ref[hh] = logits(hh, c + 2)
        return consume(carries, sb_ref, c + 1)

    carries = lax.fori_loop(0, last // 2, pair, tuple(carries))
    carries = lax.cond(last % 2 == 1, lambda cs: consume(cs, sa_ref, last - 1), lambda cs: cs, carries)
    for hh in heads:
        m, l, acc = carries[hh]
        cols = slice(hh * MOBA_DH, (hh + 1) * MOBA_DH)
        g = g_ref[:, cols]
        o_ref[:, cols] = ((g * jax.nn.sigmoid(g)) * (acc / l).T).astype(o_ref.dtype)


PAST_BLOCKS = PAST_LEN // MOBA_BLOCK


PAGE_ROWS = PAGE_SIZE * HEADS
PAGES_PER_BLOCK = MOBA_BLOCK // PAGE_SIZE


def _sample_page_copies(pt_ref, seq, slot, ck_hbm, cv_hbm, kbuf, vbuf, sem):
    copies = []
    for p in range(N_PAGES):
        page = pt_ref[seq, p]
        copies.append(pltpu.make_async_copy(ck_hbm.at[page], kbuf.at[slot, p], sem.at[0, slot]))
        copies.append(pltpu.make_async_copy(cv_hbm.at[page], vbuf.at[slot, p], sem.at[1, slot]))
    return copies


def _moba_sample_step(step, n_steps, pt_ref, x_ref, ck_hbm, cv_hbm, o_ref, kbuf, vbuf, sem):
    slot = step % 2
    copies = functools.partial(_sample_page_copies, pt_ref, ck_hbm=ck_hbm, cv_hbm=cv_hbm,
                               kbuf=kbuf, vbuf=vbuf, sem=sem)

    @pl.when(step == 0)
    def _():
        for cp in copies(0, 0):
            cp.start()

    @pl.when(step + 1 < n_steps)
    def _():
        for cp in copies(step + 1, 1 - slot):
            cp.start()

    for cp in copies(step, slot):
        cp.wait()

    q = x_ref[0:HEADS, :]
    k_new = x_ref[HEADS:2 * HEADS, :]
    v_new = x_ref[2 * HEADS:3 * HEADS, :]
    g = x_ref[3 * HEADS:4 * HEADS, :]
    qs = q * (MOBA_DH ** -0.5)
    q16 = qs.astype(BF16)

    score = []
    for j in range(PAST_BLOCKS):
        ksum = sum(jnp.sum(kbuf[slot, PAGES_PER_BLOCK * j + r].reshape(PAGE_SIZE, HEADS, MOBA_DH), axis=0)
                   for r in range(PAGES_PER_BLOCK))
        score.append(jnp.sum(q * ksum, axis=-1, keepdims=True))
    keep = []
    for j in range(PAST_BLOCKS):
        rank = jnp.zeros((HEADS, 1), F32)
        for jp in range(PAST_BLOCKS):
            if jp != j:
                beats = (score[jp] >= score[j]) if jp < j else (score[jp] > score[j])
                rank = rank + jnp.where(beats, 1.0, 0.0)
        keep.append(jnp.where(rank < MOBA_TOPK, 1.0, 0.0))
    packed = jnp.zeros((HEADS, 1), F32)
    before = jnp.zeros((HEADS, 1), F32)
    for j in range(PAST_BLOCKS):
        for t in range(MOBA_TOPK):
            packed = packed + jnp.where(before == t, keep[j] * float(j * PAST_BLOCKS ** t), 0.0)
        before = before + keep[j]
    packed = packed.astype(jnp.int32)

    def head_rows(buf, h, t):
        blk = (packed[h, 0] // (PAST_BLOCKS ** t)) % PAST_BLOCKS
        return jnp.concatenate(
            [buf[slot, PAGES_PER_BLOCK * blk + r, pl.ds(h, PAGE_SIZE, stride=HEADS), :]
             for r in range(PAGES_PER_BLOCK)], axis=0).astype(BF16)

    logits = []
    for t in range(MOBA_TOPK):
        rows = []
        for h in range(HEADS):
            s = lax.dot_general(q16, head_rows(kbuf, h, t), NT_DIMS, preferred_element_type=F32)
            rows.append(s[h:h + 1, :])
        logits.append(jnp.concatenate(rows, axis=0))
    s_self = jnp.sum(qs * k_new, axis=-1, keepdims=True)
    m = functools.reduce(jnp.maximum, [jnp.max(s, axis=-1, keepdims=True) for s in logits] + [s_self])
    probs = [jnp.exp(s - m) for s in logits]
    p_self = jnp.exp(s_self - m)
    l = functools.reduce(jnp.add, [jnp.sum(p, axis=-1, keepdims=True) for p in probs] + [p_self])
    probs16 = [p.astype(BF16) for p in probs]
    y_rows = []
    for h in range(HEADS):
        acc = jnp.zeros((1, MOBA_DH), F32)
        for t in range(MOBA_TOPK):
            r = jnp.dot(probs16[t], head_rows(vbuf, h, t), preferred_element_type=F32)
            acc = acc + r[h:h + 1, :]
        y_rows.append(acc)
    y = (jnp.concatenate(y_rows, axis=0) + p_self * v_new) / l
    o_ref[...] = (g * jax.nn.sigmoid(g)) * y


FL_STEPS_PER_GROUP = N_BLK
assert (HEADS // FL_HEADS) * FL_STEPS_PER_GROUP == DEC_BATCH


def _moba_fused_kernel(pt_ref, q_ref, k_ref, vt_ref, g_ref, x_ref, ck_hbm, cv_hbm, o_ref, os_ref,
                       sa_ref, sb_ref, kbuf, vbuf, sem):
    step = pl.program_id(0) * FL_STEPS_PER_GROUP + pl.program_id(1)
    _moba_sample_step(step, DEC_BATCH, pt_ref, x_ref, ck_hbm, cv_hbm, os_ref, kbuf, vbuf, sem)
    _moba_flash_kernel(q_ref, k_ref, vt_ref, g_ref, o_ref, sa_ref, sb_ref)


def _moba_fused(page_table, q_aug, k_aug, v_t, z_p, x_s, cache_k, cache_v):
    gw = FL_HEADS * MOBA_DH
    seq = lambda h, i: h * FL_STEPS_PER_GROUP + i
    resident = pl.Buffered(1)
    grid_spec = pltpu.PrefetchScalarGridSpec(
        num_scalar_prefetch=1,
        grid=(HEADS // FL_HEADS, N_BLK),
        in_specs=([pl.BlockSpec((FL_HEADS, MOBA_BLOCK, AUG_W), lambda h, i, pt: (h, i, 0)),
                   pl.BlockSpec((FL_HEADS, SEQ, AUG_W), lambda h, i, pt: (h, 0, 0), pipeline_mode=resident),
                   pl.BlockSpec((FL_HEADS, N_CHUNK, MOBA_DH, FL_CH), lambda h, i, pt: (h, 0, 0, 0),
                                pipeline_mode=resident),
                   pl.BlockSpec((MOBA_BLOCK, gw), lambda h, i, pt: (i, COL_GB // gw + h)),
                   pl.BlockSpec((None, 4 * HEADS, MOBA_DH), lambda h, i, pt: (seq(h, i), 0, 0)),
                   pl.BlockSpec(memory_space=pl.ANY),
                   pl.BlockSpec(memory_space=pl.ANY)]),
        out_specs=[pl.BlockSpec((MOBA_BLOCK, gw), lambda h, i, pt: (i, h)),
                   pl.BlockSpec((None, HEADS, MOBA_DH), lambda h, i, pt: (seq(h, i), 0, 0))],
        scratch_shapes=[pltpu.VMEM((FL_HEADS, FL_CH, MOBA_BLOCK), F32),
                        pltpu.VMEM((FL_HEADS, FL_CH, MOBA_BLOCK), F32),
                        pltpu.VMEM((2, N_PAGES, PAGE_ROWS, MOBA_DH), F32),
                        pltpu.VMEM((2, N_PAGES, PAGE_ROWS, MOBA_DH), F32),
                        pltpu.SemaphoreType.DMA((2, 2))],
    )
    return pl.pallas_call(
        _moba_fused_kernel,
        grid_spec=grid_spec,
        out_shape=[jax.ShapeDtypeStruct((SEQ, HEADS * MOBA_DH), BF16),
                   jax.ShapeDtypeStruct((DEC_BATCH, HEADS, MOBA_DH), F32)],
        compiler_params=_params(2),
        name="moba_fused",
    )(page_table, q_aug, k_aug, v_t, z_p, x_s, cache_k, cache_v)


def _out_proj_kernel(ga_ref, gb_ref, zma_ref, zmb_ref, x_ref, bm_ref, wa_ref, wb_ref, wo_ref, gp_ref, y_ref):
    u_a = jnp.dot(ga_ref[...], wa_ref[...], preferred_element_type=F32)
    u_b = jnp.dot(gb_ref[...], wb_ref[...], preferred_element_type=F32)
    gate_a = jax.nn.sigmoid(zma_ref[...] + bm_ref[0:1, :])
    gate_b = jax.nn.sigmoid(zmb_ref[...] + bm_ref[1:2, :])
    merged = (gate_a * u_a + gate_b * u_b).astype(BF16)
    o = jnp.dot(merged, wo_ref[...], preferred_element_type=F32)
    on = o * lax.rsqrt(jnp.mean(o * o, axis=-1, keepdims=True) + NORM_EPS)
    y_ref[...] = x_ref[...] + on * gp_ref[...]


def _out_proj(ga, gb, z, x, b_merge, w_a, w_b, w_o, g_post, tile):
    rows = x.shape[0]
    resident = lambda shape: pl.BlockSpec(shape, lambda i: (0, 0), pipeline_mode=pl.Buffered(1))
    return pl.pallas_call(
        _out_proj_kernel,
        grid=(rows // tile,),
        in_specs=[pl.BlockSpec((tile, HEADS * RET_DV), lambda i: (i, 0)),
                  pl.BlockSpec((tile, HD), lambda i: (i, 0)),
                  pl.BlockSpec((tile, D_MODEL), lambda i: (i, COL_ZM // D_MODEL)),
                  pl.BlockSpec((tile, D_MODEL), lambda i: (i, COL_ZM // D_MODEL + 1)),
                  pl.BlockSpec((tile, D_MODEL), lambda i: (i, 0)),
                  resident((2, D_MODEL)),
                  resident((HEADS * RET_DV, D_MODEL)),
                  resident((HD, D_MODEL)),
                  resident((D_MODEL, D_MODEL)),
                  resident((1, D_MODEL))],
        out_specs=pl.BlockSpec((tile, D_MODEL), lambda i: (i, 0)),
        out_shape=jax.ShapeDtypeStruct((rows, D_MODEL), F32),
        compiler_params=_params(1),
        name="out_proj",
    )(ga, gb, z, z, x, b_merge, w_a, w_b, w_o, g_post)


def _rope_tables(pos):
    half = RET_DK // 2
    inv = 1.0 / (RET_ROPE_BASE ** jnp.linspace(0.0, 1.0, half, dtype=F32))
    ang = pos.astype(F32)[:, None] * inv[None, :]
    cos, sin = jnp.cos(ang), jnp.sin(ang)
    return jnp.concatenate([cos, cos], axis=-1), jnp.concatenate([-sin, sin], axis=-1)


def kernel(x_prompt, x_sample, state_ret, cache_k, cache_v, page_table, norm_pre, w_in,
           w_a_out, w_b_out, b_merge, w_out, norm_post):
    xp = x_prompt.reshape(SEQ, D_MODEL)
    xs = x_sample.reshape(DEC_BATCH, D_MODEL)
    g_pre = norm_pre.reshape(1, D_MODEL)
    g_post = norm_post.reshape(1, D_MODEL)

    log_g = jnp.log(1.0 - 2.0 ** (-5.0 - jnp.arange(HEADS, dtype=F32)))
    dec = jnp.stack([log_g, jnp.exp(log_g * RET_CHUNK), jnp.exp(log_g * 1.0)])
    rope_row = _rope_tables(jnp.arange(RET_CHUNK, dtype=jnp.int32))
    rope_chunk = tuple(t.reshape(SEQ // RET_CHUNK, 1, RET_DK)
                       for t in _rope_tables(RET_CHUNK * jnp.arange(SEQ // RET_CHUNK, dtype=jnp.int32)))
    cos_s, sin_s = _rope_tables(PAST_LEN + jnp.arange(1, dtype=jnp.int32))

    hn_p = _rmsnorm(xp, g_pre, 1024)
    hn_s = _rmsnorm(xs, g_pre, DEC_BATCH)
    z_p, z_s = _in_proj(hn_p, hn_s, w_in.reshape(D_MODEL, PROJ_W))

    ga_p, st_p = _ret_prompt(dec, z_p, rope_row, rope_chunk)
    ga_s, st_s = _ret_sample(dec, z_s, cos_s, sin_s,
                             state_ret.reshape(DEC_BATCH, HEADS, RET_DK, RET_DV))

    q_aug, k_aug, v_t, k_rows_p, v_rows_p = _moba_prep(z_p)
    n_phys = cache_k.shape[1]
    gb_p, gb_s = _moba_fused(page_table, q_aug, k_aug, v_t, z_p,
                             z_s[:, COL_QB:COL_ZM].reshape(DEC_BATCH, 4 * HEADS, MOBA_DH),
                             cache_k.reshape(n_phys, PAGE_ROWS, MOBA_DH),
                             cache_v.reshape(n_phys, PAGE_ROWS, MOBA_DH))

    w_a = w_a_out.reshape(HEADS * RET_DV, D_MODEL).astype(BF16)
    w_b = w_b_out.reshape(HD, D_MODEL).astype(BF16)
    w_o = w_out.reshape(D_MODEL, D_MODEL).astype(BF16)
    bm = b_merge.reshape(2, D_MODEL)
    y_p = _out_proj(ga_p, gb_p, z_p, xp, bm, w_a, w_b, w_o, g_post, 256)
    y_s = _out_proj(ga_s.astype(BF16), gb_s.reshape(DEC_BATCH, HD).astype(BF16), z_s, xs, bm,
                    w_a, w_b, w_o, g_post, DEC_BATCH)

    return (y_p.reshape(1, SEQ, D_MODEL),
            y_s.reshape(DEC_BATCH, 1, D_MODEL),
            st_p.reshape(1, 1, HEADS, RET_DK, RET_DV),
            st_s.reshape(1, DEC_BATCH, HEADS, RET_DK, RET_DV),
            k_rows_p.reshape(1, 1, SEQ, HEADS, MOBA_DH),
            v_rows_p.reshape(1, 1, SEQ, HEADS, MOBA_DH),
            z_s[:, COL_KB:COL_VB].reshape(1, DEC_BATCH, 1, HEADS, MOBA_DH),
            z_s[:, COL_VB:COL_GB].reshape(1, DEC_BATCH, 1, HEADS, MOBA_DH))
```

```python
import functools

import jax
import jax.numpy as jnp
from jax import lax
from jax.experimental import pallas as pl
from jax.experimental.pallas import tpu as pltpu

F32 = jnp.float32
BF16 = jnp.bfloat16

D_MODEL = 2048
SEQ = 8192
DEC_BATCH = 128
PAST_LEN = 2048
PAGE_SIZE = 128
N_PAGES = PAST_LEN // PAGE_SIZE
HEADS = 8
RET_DK = 128
RET_DV = 256
RET_CHUNK = 128
RET_ROPE_BASE = 10000.0
MOBA_DH = 128
HD = HEADS * MOBA_DH
MOBA_BLOCK = 256
MOBA_TOPK = 3
NORM_EPS = 1e-6
NEG_INF = -1e30

COL_QA, COL_KA, COL_VA, COL_GA = 0, 1024, 2048, 4096
COL_QB, COL_KB, COL_VB, COL_GB, COL_ZM = 6144, 7168, 8192, 9216, 10240
PROJ_W = 14336

V7X_VMEM_LIMIT = 56 * 1024 * 1024

NT_DIMS = (((1,), (1,)), ((), ()))
TN_DIMS = (((0,), (0,)), ((), ()))


def _params(n_axes, vmem=V7X_VMEM_LIMIT):
    return pltpu.CompilerParams(dimension_semantics=("arbitrary",) * n_axes, vmem_limit_bytes=vmem)


def _rmsnorm_kernel(x_ref, g_ref, o_ref):
    x = x_ref[...]
    y = x * lax.rsqrt(jnp.mean(x * x, axis=-1, keepdims=True) + NORM_EPS)
    o_ref[...] = (y * g_ref[...]).astype(o_ref.dtype)


def _rmsnorm(x, gain, tile):
    rows = x.shape[0]
    return pl.pallas_call(
        _rmsnorm_kernel,
        grid=(rows // tile,),
        in_specs=[pl.BlockSpec((tile, D_MODEL), lambda i: (i, 0)),
                  pl.BlockSpec((1, D_MODEL), lambda i: (0, 0))],
        out_specs=pl.BlockSpec((tile, D_MODEL), lambda i: (i, 0)),
        out_shape=jax.ShapeDtypeStruct((rows, D_MODEL), BF16),
        compiler_params=_params(1),
        name="rmsnorm_pre",
    )(x, gain)


IN_TM = 1024
IN_TN = 1024
IN_MP = SEQ // IN_TM


def _in_proj_kernel(hp_ref, hs_ref, w_ref, zp_ref, zs_ref, wb_ref):
    m = pl.program_id(1)

    @pl.when(m == 0)
    def _():
        wb_ref[...] = w_ref[...].astype(BF16)

    @pl.when(m < IN_MP)
    def _():
        zp_ref[...] = jnp.dot(hp_ref[...], wb_ref[...], preferred_element_type=F32)

    @pl.when(m == IN_MP)
    def _():
        zs_ref[...] = jnp.dot(hs_ref[...], wb_ref[...], preferred_element_type=F32)


def _in_proj(hn_p, hn_s, w_in):
    last = IN_MP - 1
    return pl.pallas_call(
        _in_proj_kernel,
        grid=(PROJ_W // IN_TN, IN_MP + 1),
        in_specs=[pl.BlockSpec((IN_TM, D_MODEL), lambda n, m: (jnp.minimum(m, last), 0)),
                  pl.BlockSpec((DEC_BATCH, D_MODEL), lambda n, m: (0, 0)),
                  pl.BlockSpec((D_MODEL, IN_TN), lambda n, m: (0, n))],
        out_specs=[pl.BlockSpec((IN_TM, IN_TN), lambda n, m: (jnp.minimum(m, last), n)),
                   pl.BlockSpec((DEC_BATCH, IN_TN), lambda n, m: (0, n))],
        out_shape=[jax.ShapeDtypeStruct((SEQ, PROJ_W), F32),
                   jax.ShapeDtypeStruct((DEC_BATCH, PROJ_W), F32)],
        scratch_shapes=[pltpu.VMEM((D_MODEL, IN_TN), BF16)],
        compiler_params=_params(2),
        name="in_proj",
    )(hn_p, hn_s, w_in)


def _rotate(x, cos2, sin2):
    return x * cos2 + pltpu.roll(x, RET_DK // 2, 1) * sin2


def _group_norm_gate(y, g):
    yn = y * lax.rsqrt(jnp.mean(y * y, axis=-1, keepdims=True) + NORM_EPS)
    return (g * jax.nn.sigmoid(g)) * yn


def _ret_prompt_kernel(dec_ref, q_ref, k_ref, v_ref, g_ref, cos_r_ref, sin_r_ref, cos_c_ref, sin_c_ref,
                       ga_ref, st_ref, s_scr, din_scr, dq_scr, dk_scr):
    c = pl.program_id(0)
    C = RET_CHUNK

    @pl.when(c == 0)
    def _():
        ii = lax.broadcasted_iota(jnp.int32, (C, C), 0).astype(F32)
        jj = lax.broadcasted_iota(jnp.int32, (C, C), 1).astype(F32)
        diff = ii - jj
        for h in range(HEADS):
            lg = dec_ref[0, h]
            din_scr[h] = jnp.where(diff >= 0, jnp.exp(lg * jnp.maximum(diff, 0.0)), 0.0)
            dq_scr[h] = jnp.exp(lg * (ii + 1.0))
            dk_scr[h] = jnp.exp(lg * (C - 1.0 - ii))
        s_scr[...] = jnp.zeros_like(s_scr)

    cos_r, sin_r = cos_r_ref[...], sin_r_ref[...]
    cos_c, sin_c = cos_c_ref[...], sin_c_ref[...]
    cos2 = cos_c * cos_r - sin_c * sin_r
    sin2 = sin_c * cos_r + cos_c * sin_r
    for h in range(HEADS):
        qr = _rotate(q_ref[:, h * RET_DK:(h + 1) * RET_DK], cos2, sin2)
        kr = _rotate(k_ref[:, h * RET_DK:(h + 1) * RET_DK], cos2, sin2) * (RET_DK ** -0.5)
        vb = v_ref[:, h * RET_DV:(h + 1) * RET_DV].astype(BF16)
        att = lax.dot_general(qr.astype(BF16), kr.astype(BF16), NT_DIMS, preferred_element_type=F32)
        inner = jnp.dot((att * din_scr[h]).astype(BF16), vb, preferred_element_type=F32)
        s_old = s_scr[h]
        cross = jnp.dot((qr * dq_scr[h]).astype(BF16), s_old.astype(BF16), preferred_element_type=F32)
        kv = lax.dot_general((kr * dk_scr[h]).astype(BF16), vb, TN_DIMS, preferred_element_type=F32)
        s_scr[h] = dec_ref[1, h] * s_old + kv
        gated = _group_norm_gate(inner + cross, g_ref[:, h * RET_DV:(h + 1) * RET_DV])
        ga_ref[:, h * RET_DV:(h + 1) * RET_DV] = gated.astype(ga_ref.dtype)

    @pl.when(c == pl.num_programs(0) - 1)
    def _():
        st_ref[...] = s_scr[...]


RS_TB = DEC_BATCH // (SEQ // RET_CHUNK)


def _ret_sample_rows(dec_ref, q_ref, k_ref, v_ref, g_ref, cos_ref, sin_ref, s0_ref, ga_ref, s1_ref):
    cos2 = cos_ref[...]
    sin2 = sin_ref[...]
    eye = (lax.broadcasted_iota(jnp.int32, (RET_DK, RET_DK), 0)
           == lax.broadcasted_iota(jnp.int32, (RET_DK, RET_DK), 1))
    ones = jnp.ones((RET_DK, RET_DV), BF16)

    def lane_bcast_column(row):
        diag = jnp.where(eye, jnp.broadcast_to(row, (RET_DK, RET_DK)), 0.0).astype(BF16)
        return jnp.dot(diag, ones, preferred_element_type=F32)

    for h in range(HEADS):
        gamma = dec_ref[2, h]
        cols = slice(h * RET_DV, (h + 1) * RET_DV)
        qr = _rotate(q_ref[:, h * RET_DK:(h + 1) * RET_DK], cos2, sin2)
        kr = _rotate(k_ref[:, h * RET_DK:(h + 1) * RET_DK], cos2, sin2) * (RET_DK ** -0.5)
        vh = v_ref[:, cols]
        qk = jnp.sum(qr * kr, axis=-1, keepdims=True)
        inner = qk * vh
        ys = []
        for b in range(RS_TB):
            s_old = s0_ref[b, h]
            cross = jnp.sum(lane_bcast_column(qr[b:b + 1, :]) * s_old, axis=0, keepdims=True)
            s1_ref[b, h] = gamma * s_old + lane_bcast_column(kr[b:b + 1, :]) * vh[b:b + 1, :]
            ys.append(inner[b:b + 1, :] + cross * gamma)
        ga_ref[:, cols] = _group_norm_gate(jnp.concatenate(ys, axis=0), g_ref[:, cols])


def _retention_kernel(dec_ref, q_ref, k_ref, v_ref, g_ref, cos_r_ref, sin_r_ref, cos_c_ref, sin_c_ref,
                      qs_ref, ks_ref, vs_ref, gs_ref, cos_s_ref, sin_s_ref, s0_ref,
                      ga_ref, st_ref, gas_ref, s1_ref, s_scr, din_scr, dq_scr, dk_scr):
    _ret_prompt_kernel(dec_ref, q_ref, k_ref, v_ref, g_ref, cos_r_ref, sin_r_ref, cos_c_ref, sin_c_ref,
                       ga_ref, st_ref, s_scr, din_scr, dq_scr, dk_scr)
    _ret_sample_rows(dec_ref, qs_ref, ks_ref, vs_ref, gs_ref, cos_s_ref, sin_s_ref, s0_ref, gas_ref, s1_ref)


def _retention(dec, z_p, rope_row, rope_chunk, z_s, rope_s, state):
    C = RET_CHUNK
    n_steps = SEQ // C
    cos_r, sin_r = rope_row
    cos_c, sin_c = rope_chunk
    cos_s, sin_s = rope_s
    z_s3 = z_s.reshape(n_steps, RS_TB, PROJ_W)
    sample_cols = lambda width, base: pl.BlockSpec((None, RS_TB, width), lambda c, d: (c, 0, base // width))
    grid_spec = pltpu.PrefetchScalarGridSpec(
        num_scalar_prefetch=1,
        grid=(n_steps,),
        in_specs=[pl.BlockSpec((C, HEADS * RET_DK), lambda c, d: (c, COL_QA // 1024)),
                  pl.BlockSpec((C, HEADS * RET_DK), lambda c, d: (c, COL_KA // 1024)),
                  pl.BlockSpec((C, HEADS * RET_DV), lambda c, d: (c, COL_VA // 2048)),
                  pl.BlockSpec((C, HEADS * RET_DV), lambda c, d: (c, COL_GA // 2048)),
                  pl.BlockSpec((C, RET_DK), lambda c, d: (0, 0)),
                  pl.BlockSpec((C, RET_DK), lambda c, d: (0, 0)),
                  pl.BlockSpec((None, 1, RET_DK), lambda c, d: (c, 0, 0)),
                  pl.BlockSpec((None, 1, RET_DK), lambda c, d: (c, 0, 0)),
                  sample_cols(HEADS * RET_DK, COL_QA),
                  sample_cols(HEADS * RET_DK, COL_KA),
                  sample_cols(HEADS * RET_DV, COL_VA),
                  sample_cols(HEADS * RET_DV, COL_GA),
                  pl.BlockSpec((1, RET_DK), lambda c, d: (0, 0)),
                  pl.BlockSpec((1, RET_DK), lambda c, d: (0, 0)),
                  pl.BlockSpec((RS_TB, HEADS, RET_DK, RET_DV), lambda c, d: (c, 0, 0, 0))],
        out_specs=[pl.BlockSpec((C, HEADS * RET_DV), lambda c, d: (c, 0)),
                   pl.BlockSpec((HEADS, RET_DK, RET_DV), lambda c, d: (0, 0, 0)),
                   pl.BlockSpec((None, RS_TB, HEADS * RET_DV), lambda c, d: (c, 0, 0)),
                   pl.BlockSpec((RS_TB, HEADS, RET_DK, RET_DV), lambda c, d: (c, 0, 0, 0))],
        scratch_shapes=[pltpu.VMEM((HEADS, RET_DK, RET_DV), F32),
                        pltpu.VMEM((HEADS, C, C), F32),
                        pltpu.VMEM((HEADS, C, C), F32),
                        pltpu.VMEM((HEADS, C, C), F32)],
    )
    ga_p, st_p, ga_s, st_s = pl.pallas_call(
        _retention_kernel,
        grid_spec=grid_spec,
        out_shape=[jax.ShapeDtypeStruct((SEQ, HEADS * RET_DV), BF16),
                   jax.ShapeDtypeStruct((HEADS, RET_DK, RET_DV), F32),
                   jax.ShapeDtypeStruct((n_steps, RS_TB, HEADS * RET_DV), F32),
                   jax.ShapeDtypeStruct((DEC_BATCH, HEADS, RET_DK, RET_DV), F32)],
        compiler_params=_params(1),
        name="retention",
    )(dec, z_p, z_p, z_p, z_p, cos_r, sin_r, cos_c, sin_c, z_s3, z_s3, z_s3, z_s3, cos_s, sin_s, state)
    return ga_p, st_p, ga_s.reshape(DEC_BATCH, HEADS * RET_DV), st_s


N_BLK = SEQ // MOBA_BLOCK
BIAS_W = 128
AUG_W = MOBA_DH + BIAS_W


FL_KC = 4
FL_CH = FL_KC * MOBA_BLOCK
N_CHUNK = SEQ // FL_CH
LOG2E = 1.4426950408889634
Q_SCALE = (MOBA_DH ** -0.5) * LOG2E


def _row_output_copies(i, k_ref, v_ref, kr_hbm, vr_hbm, sem):
    rows = pl.ds(pl.multiple_of(i * MOBA_BLOCK, MOBA_BLOCK), MOBA_BLOCK)
    copies = []
    for h in range(HEADS):
        cols = pl.ds(h * MOBA_DH, MOBA_DH)
        copies.append(pltpu.make_async_copy(k_ref.at[:, cols], kr_hbm.at[rows, h, :], sem.at[0]))
        copies.append(pltpu.make_async_copy(v_ref.at[:, cols], vr_hbm.at[rows, h, :], sem.at[1]))
    return copies


def _moba_prep_kernel(q_ref, k_ref, v_ref, qa_ref, ka_ref, vt_ref, kr_hbm, vr_hbm, km_scr, sem):
    i = pl.program_id(0)
    row_copies = _row_output_copies(i, k_ref, v_ref, kr_hbm, vr_hbm, sem)
    for cp in row_copies:
        cp.start()

    @pl.when(i == 0)
    def _():
        km_scr[...] = jnp.zeros_like(km_scr)

    blk = lax.broadcasted_iota(jnp.int32, (N_BLK, MOBA_BLOCK), 0)
    sub8 = lax.broadcasted_iota(jnp.int32, (8, MOBA_BLOCK), 0)
    lane = lax.broadcasted_iota(jnp.int32, (MOBA_BLOCK, BIAS_W), 1)
    onehot = jnp.where(lane == i, 1.0, 0.0).astype(BF16)
    pad = jnp.full((BIAS_W - N_BLK, MOBA_BLOCK), NEG_INF, F32)
    for h in range(HEADS):
        cols = slice(h * MOBA_DH, (h + 1) * MOBA_DH)
        q = q_ref[:, cols]
        k = k_ref[:, cols]
        s_t = lax.dot_general(km_scr[h], q, NT_DIMS, precision=lax.Precision.HIGHEST,
                              preferred_element_type=F32)
        groups = [s_t[r:r + 8, :] for r in range(0, N_BLK, 8)]
        ranks = [jnp.zeros((8, MOBA_BLOCK), F32) for _ in groups]
        for jp in range(N_BLK):
            sj = s_t[jp:jp + 1, :]
            is_past = jnp.where(jp < i, 1.0, 0.0)
            g0, r0 = divmod(jp, 8)
            for gi, grp in enumerate(groups):
                if gi < g0:
                    hit = jnp.where(sj > grp, is_past, 0.0)
                elif gi > g0:
                    hit = jnp.where(sj >= grp, is_past, 0.0)
                else:
                    hit = jnp.where(sub8 > r0, jnp.where(sj >= grp, is_past, 0.0),
                                    jnp.where(sj > grp, is_past, 0.0))
                ranks[gi] = ranks[gi] + hit
        rank = jnp.concatenate(ranks, axis=0)
        keep = ((blk < i) & (rank < MOBA_TOPK)) | (blk == i)
        bias_t = jnp.concatenate([jnp.where(keep, 0.0, NEG_INF), pad], axis=0)
        qa_ref[h, :, :MOBA_DH] = (q * Q_SCALE).astype(BF16)
        qa_ref[h, :, MOBA_DH:] = bias_t.T.astype(BF16)
        ka_ref[h, :, :MOBA_DH] = k.astype(BF16)
        ka_ref[h, :, MOBA_DH:] = onehot
        v = v_ref[:, cols]
        vt_ref[h] = v.T.astype(BF16)
        km_scr[h, pl.ds(i, 1), :] = jnp.mean(k, axis=0, keepdims=True)
    for cp in row_copies:
        cp.wait()


def _moba_prep(z_p):
    col = lambda base: (lambda i: (i, base // HD))
    out_idx = lambda i: (0, i, 0)
    return pl.pallas_call(
        _moba_prep_kernel,
        grid=(N_BLK,),
        in_specs=[pl.BlockSpec((MOBA_BLOCK, HD), col(COL_QB)),
                  pl.BlockSpec((MOBA_BLOCK, HD), col(COL_KB)),
                  pl.BlockSpec((MOBA_BLOCK, HD), col(COL_VB))],
        out_specs=[pl.BlockSpec((HEADS, MOBA_BLOCK, AUG_W), out_idx),
                   pl.BlockSpec((HEADS, MOBA_BLOCK, AUG_W), out_idx),
                   pl.BlockSpec((HEADS, None, MOBA_DH, MOBA_BLOCK), lambda i: (0, i // FL_KC, 0, i % FL_KC)),
                   pl.BlockSpec(memory_space=pl.ANY),
                   pl.BlockSpec(memory_space=pl.ANY)],
        out_shape=[jax.ShapeDtypeStruct((HEADS, SEQ, AUG_W), BF16),
                   jax.ShapeDtypeStruct((HEADS, SEQ, AUG_W), BF16),
                   jax.ShapeDtypeStruct((HEADS, N_CHUNK, MOBA_DH, FL_CH), BF16),
                   jax.ShapeDtypeStruct((SEQ, HEADS, MOBA_DH), F32),
                   jax.ShapeDtypeStruct((SEQ, HEADS, MOBA_DH), F32)],
        scratch_shapes=[pltpu.VMEM((HEADS, N_BLK, MOBA_DH), F32), pltpu.SemaphoreType.DMA((2,))],
        compiler_params=_params(1),
        name="moba_prep",
    )(z_p, z_p, z_p)


FL_HEADS = 2


def _moba_flash_kernel(q_ref, k_ref, vt_ref, g_ref, o_ref, sa_ref, sb_ref):
    i = pl.program_id(1)
    last = i // FL_KC
    heads = range(FL_HEADS)

    def logits(hh, c):
        rows = pl.ds(pl.multiple_of(c * FL_CH, FL_CH), FL_CH)
        return lax.dot_general(k_ref[hh, rows, :], q_ref[hh], NT_DIMS, preferred_element_type=F32)

    def update(carry, s_t, vt):
        m, l, acc = carry
        m_new = jnp.maximum(m, jnp.max(s_t, axis=0, keepdims=True))
        alpha = jnp.exp2(m - m_new)
        p = jnp.exp2(s_t - m_new)
        l = alpha * l + jnp.sum(p, axis=0, keepdims=True)
        acc = alpha * acc + jnp.dot(vt, p.astype(BF16), preferred_element_type=F32)
        return m_new, l, acc

    for hh in heads:
        sa_ref[hh] = logits(hh, 0)

    def own_chunk(n_blocks):
        n = n_blocks * MOBA_BLOCK

        def run():
            rows = pl.ds(pl.multiple_of(last * FL_CH, FL_CH), n)
            kpos = last * FL_CH + lax.broadcasted_iota(jnp.int32, (n, MOBA_BLOCK), 0)
            qpos = i * MOBA_BLOCK + lax.broadcasted_iota(jnp.int32, (n, MOBA_BLOCK), 1)
            out = []
            for hh in heads:
                s_t = lax.dot_general(k_ref[hh, rows, :], q_ref[hh], NT_DIMS, preferred_element_type=F32)
                s_t = jnp.where(kpos <= qpos, s_t, NEG_INF)
                m = jnp.max(s_t, axis=0, keepdims=True)
                p = jnp.exp2(s_t - m)
                l = jnp.sum(p, axis=0, keepdims=True)
                acc = jnp.dot(vt_ref[hh, last, :, 0:n], p.astype(BF16), preferred_element_type=F32)
                out.append((m, l, acc))
            return tuple(out)
        return run

    carries = lax.switch(i % FL_KC, [own_chunk(b + 1) for b in range(FL_KC)])

    def consume(carries, s_ref, c):
        return tuple(update(carries[hh], s_ref[hh], vt_ref[hh, c]) for hh in heads)

    def pair(t, carries):
        c = 2 * t
        for hh in heads:
            sb_ref[hh] = logits(hh, c + 1)
        carries = consume(carries, sa_ref, c)
        for hh in heads:
            sa_ref[hh] = logits(hh, c + 2)
        return consume(carries, sb_ref, c + 1)

    carries = lax.fori_loop(0, last // 2, pair, tuple(carries))
    carries = lax.cond(last % 2 == 1, lambda cs: consume(cs, sa_ref, last - 1), lambda cs: cs, carries)
    for hh in heads:
        m, l, acc = carries[hh]
        cols = slice(hh * MOBA_DH, (hh + 1) * MOBA_DH)
        g = g_ref[:, cols]
        o_ref[:, cols] = ((g * jax.nn.sigmoid(g)) * (acc / l).T).astype(o_ref.dtype)


PAST_BLOCKS = PAST_LEN // MOBA_BLOCK


PAGE_ROWS = PAGE_SIZE * HEADS
PAGES_PER_BLOCK = MOBA_BLOCK // PAGE_SIZE


def _sample_page_copies(pt_ref, seq, slot, ck_hbm, cv_hbm, kbuf, vbuf, sem):
    copies = []
    for p in range(N_PAGES):
        page = pt_ref[seq, p]
        copies.append(pltpu.make_async_copy(ck_hbm.at[page], kbuf.at[slot, p], sem.at[0, slot]))
        copies.append(pltpu.make_async_copy(cv_hbm.at[page], vbuf.at[slot, p], sem.at[1, slot]))
    return copies


def _moba_sample_step(step, n_steps, pt_ref, x_ref, ck_hbm, cv_hbm, o_ref, kbuf, vbuf, sem):
    slot = step % 2
    copies = functools.partial(_sample_page_copies, pt_ref, ck_hbm=ck_hbm, cv_hbm=cv_hbm,
                               kbuf=kbuf, vbuf=vbuf, sem=sem)

    @pl.when(step == 0)
    def _():
        for cp in copies(0, 0):
            cp.start()

    @pl.when(step + 1 < n_steps)
    def _():
        for cp in copies(step + 1, 1 - slot):
            cp.start()

    for cp in copies(step, slot):
        cp.wait()

    q = x_ref[0:HEADS, :]
    k_new = x_ref[HEADS:2 * HEADS, :]
    v_new = x_ref[2 * HEADS:3 * HEADS, :]
    g = x_ref[3 * HEADS:4 * HEADS, :]
    qs = q * (MOBA_DH ** -0.5)
    q16 = qs.astype(BF16)

    score = []
    for j in range(PAST_BLOCKS):
        ksum = sum(jnp.sum(kbuf[slot, PAGES_PER_BLOCK * j + r].reshape(PAGE_SIZE, HEADS, MOBA_DH), axis=0)
                   for r in range(PAGES_PER_BLOCK))
        score.append(jnp.sum(q * ksum, axis=-1, keepdims=True))
    keep = []
    for j in range(PAST_BLOCKS):
        rank = jnp.zeros((HEADS, 1), F32)
        for jp in range(PAST_BLOCKS):
            if jp != j:
                beats = (score[jp] >= score[j]) if jp < j else (score[jp] > score[j])
                rank = rank + jnp.where(beats, 1.0, 0.0)
        keep.append(jnp.where(rank < MOBA_TOPK, 1.0, 0.0))
    packed = jnp.zeros((HEADS, 1), F32)
    before = jnp.zeros((HEADS, 1), F32)
    for j in range(PAST_BLOCKS):
        for t in range(MOBA_TOPK):
            packed = packed + jnp.where(before == t, keep[j] * float(j * PAST_BLOCKS ** t), 0.0)
        before = before + keep[j]
    packed = packed.astype(jnp.int32)

    def head_rows(buf, h, t):
        blk = (packed[h, 0] // (PAST_BLOCKS ** t)) % PAST_BLOCKS
        return jnp.concatenate(
            [buf[slot, PAGES_PER_BLOCK * blk + r, pl.ds(h, PAGE_SIZE, stride=HEADS), :]
             for r in range(PAGES_PER_BLOCK)], axis=0).astype(BF16)

    logits = []
    for t in range(MOBA_TOPK):
        rows = []
        for h in range(HEADS):
            s = lax.dot_general(q16, head_rows(kbuf, h, t), NT_DIMS, preferred_element_type=F32)
            rows.append(s[h:h + 1, :])
        logits.append(jnp.concatenate(rows, axis=0))
    s_self = jnp.sum(qs * k_new, axis=-1, keepdims=True)
    m = functools.reduce(jnp.maximum, [jnp.max(s, axis=-1, keepdims=True) for s in logits] + [s_self])
    probs = [jnp.exp(s - m) for s in logits]
    p_self = jnp.exp(s_self - m)
    l = functools.reduce(jnp.add, [jnp.sum(p, axis=-1, keepdims=True) for p in probs] + [p_self])
    probs16 = [p.astype(BF16) for p in probs]
    y_rows = []
    for h in range(HEADS):
        acc = jnp.zeros((1, MOBA_DH), F32)
        for t in range(MOBA_TOPK):
            r = jnp.dot(probs16[t], head_rows(vbuf, h, t), preferred_element_type=F32)
            acc = acc + r[h:h + 1, :]
        y_rows.append(acc)
    y = (jnp.concatenate(y_rows, axis=0) + p_self * v_new) / l
    o_ref[...] = (g * jax.nn.sigmoid(g)) * y


FL_STEPS_PER_GROUP = N_BLK
assert (HEADS // FL_HEADS) * FL_STEPS_PER_GROUP == DEC_BATCH


def _moba_fused_kernel(pt_ref, q_ref, k_ref, vt_ref, g_ref, x_ref, ck_hbm, cv_hbm, o_ref, os_ref,
                       sa_ref, sb_ref, kbuf, vbuf, sem):
    step = pl.program_id(0) * FL_STEPS_PER_GROUP + pl.program_id(1)
    _moba_sample_step(step, DEC_BATCH, pt_ref, x_ref, ck_hbm, cv_hbm, os_ref, kbuf, vbuf, sem)
    _moba_flash_kernel(q_ref, k_ref, vt_ref, g_ref, o_ref, sa_ref, sb_ref)


def _moba_fused(page_table, q_aug, k_aug, v_t, z_p, x_s, cache_k, cache_v):
    gw = FL_HEADS * MOBA_DH
    seq = lambda h, i: h * FL_STEPS_PER_GROUP + i
    resident = pl.Buffered(1)
    grid_spec = pltpu.PrefetchScalarGridSpec(
        num_scalar_prefetch=1,
        grid=(HEADS // FL_HEADS, N_BLK),
        in_specs=([pl.BlockSpec((FL_HEADS, MOBA_BLOCK, AUG_W), lambda h, i, pt: (h, i, 0)),
                   pl.BlockSpec((FL_HEADS, SEQ, AUG_W), lambda h, i, pt: (h, 0, 0), pipeline_mode=resident),
                   pl.BlockSpec((FL_HEADS, N_CHUNK, MOBA_DH, FL_CH), lambda h, i, pt: (h, 0, 0, 0),
                                pipeline_mode=resident),
                   pl.BlockSpec((MOBA_BLOCK, gw), lambda h, i, pt: (i, COL_GB // gw + h)),
                   pl.BlockSpec((None, 4 * HEADS, MOBA_DH), lambda h, i, pt: (seq(h, i), 0, 0)),
                   pl.BlockSpec(memory_space=pl.ANY),
                   pl.BlockSpec(memory_space=pl.ANY)]),
        out_specs=[pl.BlockSpec((MOBA_BLOCK, gw), lambda h, i, pt: (i, h)),
                   pl.BlockSpec((None, HEADS, MOBA_DH), lambda h, i, pt: (seq(h, i), 0, 0))],
        scratch_shapes=[pltpu.VMEM((FL_HEADS, FL_CH, MOBA_BLOCK), F32),
                        pltpu.VMEM((FL_HEADS, FL_CH, MOBA_BLOCK), F32),
                        pltpu.VMEM((2, N_PAGES, PAGE_ROWS, MOBA_DH), F32),
                        pltpu.VMEM((2, N_PAGES, PAGE_ROWS, MOBA_DH), F32),
                        pltpu.SemaphoreType.DMA((2, 2))],
    )
    return pl.pallas_call(
        _moba_fused_kernel,
        grid_spec=grid_spec,
        out_shape=[jax.ShapeDtypeStruct((SEQ, HEADS * MOBA_DH), BF16),
                   jax.ShapeDtypeStruct((DEC_BATCH, HEADS, MOBA_DH), F32)],
        compiler_params=_params(2),
        name="moba_fused",
    )(page_table, q_aug, k_aug, v_t, z_p, x_s, cache_k, cache_v)


def _out_proj_kernel(ga_ref, gb_ref, zma_ref, zmb_ref, x_ref, bm_ref, wa_ref, wb_ref, wo_ref, gp_ref, y_ref):
    u_a = jnp.dot(ga_ref[...], wa_ref[...], preferred_element_type=F32)
    u_b = jnp.dot(gb_ref[...], wb_ref[...], preferred_element_type=F32)
    gate_a = jax.nn.sigmoid(zma_ref[...] + bm_ref[0:1, :])
    gate_b = jax.nn.sigmoid(zmb_ref[...] + bm_ref[1:2, :])
    merged = (gate_a * u_a + gate_b * u_b).astype(BF16)
    o = jnp.dot(merged, wo_ref[...], preferred_element_type=F32)
    on = o * lax.rsqrt(jnp.mean(o * o, axis=-1, keepdims=True) + NORM_EPS)
    y_ref[...] = x_ref[...] + on * gp_ref[...]


def _out_proj(ga, gb, z, x, b_merge, w_a, w_b, w_o, g_post, tile):
    rows = x.shape[0]
    resident = lambda shape: pl.BlockSpec(shape, lambda i: (0, 0), pipeline_mode=pl.Buffered(1))
    return pl.pallas_call(
        _out_proj_kernel,
        grid=(rows // tile,),
        in_specs=[pl.BlockSpec((tile, HEADS * RET_DV), lambda i: (i, 0)),
                  pl.BlockSpec((tile, HD), lambda i: (i, 0)),
                  pl.BlockSpec((tile, D_MODEL), lambda i: (i, COL_ZM // D_MODEL)),
                  pl.BlockSpec((tile, D_MODEL), lambda i: (i, COL_ZM // D_MODEL + 1)),
                  pl.BlockSpec((tile, D_MODEL), lambda i: (i, 0)),
                  resident((2, D_MODEL)),
                  resident((HEADS * RET_DV, D_MODEL)),
                  resident((HD, D_MODEL)),
                  resident((D_MODEL, D_MODEL)),
                  resident((1, D_MODEL))],
        out_specs=pl.BlockSpec((tile, D_MODEL), lambda i: (i, 0)),
        out_shape=jax.ShapeDtypeStruct((rows, D_MODEL), F32),
        compiler_params=_params(1),
        name="out_proj",
    )(ga, gb, z, z, x, b_merge, w_a, w_b, w_o, g_post)


def _rope_tables(pos):
    half = RET_DK // 2
    inv = 1.0 / (RET_ROPE_BASE ** jnp.linspace(0.0, 1.0, half, dtype=F32))
    ang = pos.astype(F32)[:, None] * inv[None, :]
    cos, sin = jnp.cos(ang), jnp.sin(ang)
    return jnp.concatenate([cos, cos], axis=-1), jnp.concatenate([-sin, sin], axis=-1)


def kernel(x_prompt, x_sample, state_ret, cache_k, cache_v, page_table, norm_pre, w_in,
           w_a_out, w_b_out, b_merge, w_out, norm_post):
    xp = x_prompt.reshape(SEQ, D_MODEL)
    xs = x_sample.reshape(DEC_BATCH, D_MODEL)
    g_pre = norm_pre.reshape(1, D_MODEL)
    g_post = norm_post.reshape(1, D_MODEL)

    log_g = jnp.log(1.0 - 2.0 ** (-5.0 - jnp.arange(HEADS, dtype=F32)))
    dec = jnp.stack([log_g, jnp.exp(log_g * RET_CHUNK), jnp.exp(log_g * 1.0)])
    rope_row = _rope_tables(jnp.arange(RET_CHUNK, dtype=jnp.int32))
    rope_chunk = tuple(t.reshape(SEQ // RET_CHUNK, 1, RET_DK)
                       for t in _rope_tables(RET_CHUNK * jnp.arange(SEQ // RET_CHUNK, dtype=jnp.int32)))
    cos_s, sin_s = _rope_tables(PAST_LEN + jnp.arange(1, dtype=jnp.int32))

    hn_p = _rmsnorm(xp, g_pre, 1024)
    hn_s = _rmsnorm(xs, g_pre, DEC_BATCH)
    z_p, z_s = _in_proj(hn_p, hn_s, w_in.reshape(D_MODEL, PROJ_W))

    ga_p, st_p, ga_s, st_s = _retention(dec, z_p, rope_row, rope_chunk, z_s, (cos_s, sin_s),
                                        state_ret.reshape(DEC_BATCH, HEADS, RET_DK, RET_DV))

    q_aug, k_aug, v_t, k_rows_p, v_rows_p = _moba_prep(z_p)
    n_phys = cache_k.shape[1]
    gb_p, gb_s = _moba_fused(page_table, q_aug, k_aug, v_t, z_p,
                             z_s[:, COL_QB:COL_ZM].reshape(DEC_BATCH, 4 * HEADS, MOBA_DH),
                             cache_k.reshape(n_phys, PAGE_ROWS, MOBA_DH),
                             cache_v.reshape(n_phys, PAGE_ROWS, MOBA_DH))

    w_a = w_a_out.reshape(HEADS * RET_DV, D_MODEL).astype(BF16)
    w_b = w_b_out.reshape(HD, D_MODEL).astype(BF16)
    w_o = w_out.reshape(D_MODEL, D_MODEL).astype(BF16)
    bm = b_merge.reshape(2, D_MODEL)
    y_p = _out_proj(ga_p, gb_p, z_p, xp, bm, w_a, w_b, w_o, g_post, 256)
    y_s = _out_proj(ga_s.astype(BF16), gb_s.reshape(DEC_BATCH, HD).astype(BF16), z_s, xs, bm,
                    w_a, w_b, w_o, g_post, DEC_BATCH)

    return (y_p.reshape(1, SEQ, D_MODEL),
            y_s.reshape(DEC_BATCH, 1, D_MODEL),
            st_p.reshape(1, 1, HEADS, RET_DK, RET_DV),
            st_s.reshape(1, DEC_BATCH, HEADS, RET_DK, RET_DV),
            k_rows_p.reshape(1, 1, SEQ, HEADS, MOBA_DH),
            v_rows_p.reshape(1, 1, SEQ, HEADS, MOBA_DH),
            z_s[:, COL_KB:COL_VB].reshape(1, DEC_BATCH, 1, HEADS, MOBA_DH),
            z_s[:, COL_VB:COL_GB].reshape(1, DEC_BATCH, 1, HEADS, MOBA_DH))
```

```python
import functools

import jax
import jax.numpy as jnp
from jax import lax
from jax.experimental import pallas as pl
from jax.experimental.pallas import tpu as pltpu

F32 = jnp.float32
BF16 = jnp.bfloat16

D_MODEL = 2048
SEQ = 8192
DEC_BATCH = 128
PAST_LEN = 2048
PAGE_SIZE = 128
N_PAGES = PAST_LEN // PAGE_SIZE
HEADS = 8
RET_DK = 128
RET_DV = 256
RET_CHUNK = 128
RET_ROPE_BASE = 10000.0
MOBA_DH = 128
HD = HEADS * MOBA_DH
MOBA_BLOCK = 256
MOBA_TOPK = 3
NORM_EPS = 1e-6
NEG_INF = -1e30

COL_QA, COL_KA, COL_VA, COL_GA = 0, 1024, 2048, 4096
COL_QB, COL_KB, COL_VB, COL_GB, COL_ZM = 6144, 7168, 8192, 9216, 10240
PROJ_W = 14336

V7X_VMEM_LIMIT = 56 * 1024 * 1024

NT_DIMS = (((1,), (1,)), ((), ()))
TN_DIMS = (((0,), (0,)), ((), ()))


def _params(n_axes, vmem=V7X_VMEM_LIMIT):
    return pltpu.CompilerParams(dimension_semantics=("arbitrary",) * n_axes, vmem_limit_bytes=vmem)


def _rmsnorm_kernel(x_ref, g_ref, o_ref):
    x = x_ref[...]
    y = x * lax.rsqrt(jnp.mean(x * x, axis=-1, keepdims=True) + NORM_EPS)
    o_ref[...] = (y * g_ref[...]).astype(o_ref.dtype)


def _rmsnorm(x, gain, tile):
    rows = x.shape[0]
    return pl.pallas_call(
        _rmsnorm_kernel,
        grid=(rows // tile,),
        in_specs=[pl.BlockSpec((tile, D_MODEL), lambda i: (i, 0)),
                  pl.BlockSpec((1, D_MODEL), lambda i: (0, 0))],
        out_specs=pl.BlockSpec((tile, D_MODEL), lambda i: (i, 0)),
        out_shape=jax.ShapeDtypeStruct((rows, D_MODEL), BF16),
        compiler_params=_params(1),
        name="rmsnorm_pre",
    )(x, gain)


IN_TM = 1024
IN_TN = 1024
IN_MP = SEQ // IN_TM


def _in_proj_kernel(hp_ref, hs_ref, w_ref, zp_ref, zs_ref, wb_ref):
    m = pl.program_id(1)

    @pl.when(m == 0)
    def _():
        wb_ref[...] = w_ref[...].astype(BF16)

    @pl.when(m < IN_MP)
    def _():
        zp_ref[...] = jnp.dot(hp_ref[...], wb_ref[...], preferred_element_type=F32)

    @pl.when(m == IN_MP)
    def _():
        zs_ref[...] = jnp.dot(hs_ref[...], wb_ref[...], preferred_element_type=F32)


def _in_proj(hn_p, hn_s, w_in):
    last = IN_MP - 1
    return pl.pallas_call(
        _in_proj_kernel,
        grid=(PROJ_W // IN_TN, IN_MP + 1),
        in_specs=[pl.BlockSpec((IN_TM, D_MODEL), lambda n, m: (jnp.minimum(m, last), 0)),
                  pl.BlockSpec((DEC_BATCH, D_MODEL), lambda n, m: (0, 0)),
                  pl.BlockSpec((D_MODEL, IN_TN), lambda n, m: (0, n))],
        out_specs=[pl.BlockSpec((IN_TM, IN_TN), lambda n, m: (jnp.minimum(m, last), n)),
                   pl.BlockSpec((DEC_BATCH, IN_TN), lambda n, m: (0, n))],
        out_shape=[jax.ShapeDtypeStruct((SEQ, PROJ_W), F32),
                   jax.ShapeDtypeStruct((DEC_BATCH, PROJ_W), F32)],
        scratch_shapes=[pltpu.VMEM((D_MODEL, IN_TN), BF16)],
        compiler_params=_params(2),
        name="in_proj",
    )(hn_p, hn_s, w_in)


def _rotate(x, cos2, sin2):
    return x * cos2 + pltpu.roll(x, RET_DK // 2, 1) * sin2


def _group_norm_gate(y, g):
    yn = y * lax.rsqrt(jnp.mean(y * y, axis=-1, keepdims=True) + NORM_EPS)
    return (g * jax.nn.sigmoid(g)) * yn


def _ret_prompt_kernel(dec_ref, q_ref, k_ref, v_ref, g_ref, cos_r_ref, sin_r_ref, cos_c_ref, sin_c_ref,
                       ga_ref, st_ref, s_scr, din_scr, dq_scr, dk_scr):
    c = pl.program_id(0)
    C = RET_CHUNK

    @pl.when(c == 0)
    def _():
        ii = lax.broadcasted_iota(jnp.int32, (C, C), 0).astype(F32)
        jj = lax.broadcasted_iota(jnp.int32, (C, C), 1).astype(F32)
        diff = ii - jj
        for h in range(HEADS):
            lg = dec_ref[0, h]
            din_scr[h] = jnp.where(diff >= 0, jnp.exp(lg * jnp.maximum(diff, 0.0)), 0.0)
            dq_scr[h] = jnp.exp(lg * (ii + 1.0))
            dk_scr[h] = jnp.exp(lg * (C - 1.0 - ii))
        s_scr[...] = jnp.zeros_like(s_scr)

    cos_r, sin_r = cos_r_ref[...], sin_r_ref[...]
    cos_c, sin_c = cos_c_ref[...], sin_c_ref[...]
    cos2 = cos_c * cos_r - sin_c * sin_r
    sin2 = sin_c * cos_r + cos_c * sin_r
    for h in range(HEADS):
        qr = _rotate(q_ref[:, h * RET_DK:(h + 1) * RET_DK], cos2, sin2)
        kr = _rotate(k_ref[:, h * RET_DK:(h + 1) * RET_DK], cos2, sin2) * (RET_DK ** -0.5)
        vb = v_ref[:, h * RET_DV:(h + 1) * RET_DV].astype(BF16)
        att = lax.dot_general(qr.astype(BF16), kr.astype(BF16), NT_DIMS, preferred_element_type=F32)
        inner = jnp.dot((att * din_scr[h]).astype(BF16), vb, preferred_element_type=F32)
        s_old = s_scr[h]
        cross = jnp.dot((qr * dq_scr[h]).astype(BF16), s_old.astype(BF16), preferred_element_type=F32)
        kv = lax.dot_general((kr * dk_scr[h]).astype(BF16), vb, TN_DIMS, preferred_element_type=F32)
        s_scr[h] = dec_ref[1, h] * s_old + kv
        gated = _group_norm_gate(inner + cross, g_ref[:, h * RET_DV:(h + 1) * RET_DV])
        ga_ref[:, h * RET_DV:(h + 1) * RET_DV] = gated.astype(ga_ref.dtype)

    @pl.when(c == pl.num_programs(0) - 1)
    def _():
        st_ref[...] = s_scr[...]


RS_TB = DEC_BATCH // (SEQ // RET_CHUNK)


def _ret_sample_rows(dec_ref, q_ref, k_ref, v_ref, g_ref, cos_ref, sin_ref, s0_ref, ga_ref, s1_ref):
    cos2 = cos_ref[...]
    sin2 = sin_ref[...]
    eye = (lax.broadcasted_iota(jnp.int32, (RET_DK, RET_DK), 0)
           == lax.broadcasted_iota(jnp.int32, (RET_DK, RET_DK), 1))
    ones = jnp.ones((RET_DK, RET_DV), BF16)

    def lane_bcast_column(row):
        diag = jnp.where(eye, jnp.broadcast_to(row, (RET_DK, RET_DK)), 0.0).astype(BF16)
        return jnp.dot(diag, ones, preferred_element_type=F32)

    for h in range(HEADS):
        gamma = dec_ref[2, h]
        cols = slice(h * RET_DV, (h + 1) * RET_DV)
        qr = _rotate(q_ref[:, h * RET_DK:(h + 1) * RET_DK], cos2, sin2)
        kr = _rotate(k_ref[:, h * RET_DK:(h + 1) * RET_DK], cos2, sin2) * (RET_DK ** -0.5)
        vh = v_ref[:, cols]
        qk = jnp.sum(qr * kr, axis=-1, keepdims=True)
        inner = qk * vh
        ys = []
        for b in range(RS_TB):
            s_old = s0_ref[b, h]
            cross = jnp.sum(lane_bcast_column(qr[b:b + 1, :]) * s_old, axis=0, keepdims=True)
            s1_ref[b, h] = gamma * s_old + lane_bcast_column(kr[b:b + 1, :]) * vh[b:b + 1, :]
            ys.append(inner[b:b + 1, :] + cross * gamma)
        ga_ref[:, cols] = _group_norm_gate(jnp.concatenate(ys, axis=0), g_ref[:, cols])


def _retention_kernel(dec_ref, q_ref, k_ref, v_ref, g_ref, cos_r_ref, sin_r_ref, cos_c_ref, sin_c_ref,
                      qs_ref, ks_ref, vs_ref, gs_ref, cos_s_ref, sin_s_ref, s0_ref,
                      ga_ref, st_ref, gas_ref, s1_ref, s_scr, din_scr, dq_scr, dk_scr):
    _ret_prompt_kernel(dec_ref, q_ref, k_ref, v_ref, g_ref, cos_r_ref, sin_r_ref, cos_c_ref, sin_c_ref,
                       ga_ref, st_ref, s_scr, din_scr, dq_scr, dk_scr)
    _ret_sample_rows(dec_ref, qs_ref, ks_ref, vs_ref, gs_ref, cos_s_ref, sin_s_ref, s0_ref, gas_ref, s1_ref)


def _retention(dec, z_p, rope_row, rope_chunk, z_s, rope_s, state):
    C = RET_CHUNK
    n_steps = SEQ // C
    cos_r, sin_r = rope_row
    cos_c, sin_c = rope_chunk
    cos_s, sin_s = rope_s
    z_s3 = z_s[:, :COL_QB].reshape(n_steps, RS_TB, COL_QB)
    sample_cols = lambda width, base: pl.BlockSpec((None, RS_TB, width), lambda c, d: (c, 0, base // width))
    grid_spec = pltpu.PrefetchScalarGridSpec(
        num_scalar_prefetch=1,
        grid=(n_steps,),
        in_specs=[pl.BlockSpec((C, HEADS * RET_DK), lambda c, d: (c, COL_QA // 1024)),
                  pl.BlockSpec((C, HEADS * RET_DK), lambda c, d: (c, COL_KA // 1024)),
                  pl.BlockSpec((C, HEADS * RET_DV), lambda c, d: (c, COL_VA // 2048)),
                  pl.BlockSpec((C, HEADS * RET_DV), lambda c, d: (c, COL_GA // 2048)),
                  pl.BlockSpec((C, RET_DK), lambda c, d: (0, 0)),
                  pl.BlockSpec((C, RET_DK), lambda c, d: (0, 0)),
                  pl.BlockSpec((None, 1, RET_DK), lambda c, d: (c, 0, 0)),
                  pl.BlockSpec((None, 1, RET_DK), lambda c, d: (c, 0, 0)),
                  sample_cols(HEADS * RET_DK, COL_QA),
                  sample_cols(HEADS * RET_DK, COL_KA),
                  sample_cols(HEADS * RET_DV, COL_VA),
                  sample_cols(HEADS * RET_DV, COL_GA),
                  pl.BlockSpec((1, RET_DK), lambda c, d: (0, 0)),
                  pl.BlockSpec((1, RET_DK), lambda c, d: (0, 0)),
                  pl.BlockSpec((RS_TB, HEADS, RET_DK, RET_DV), lambda c, d: (c, 0, 0, 0))],
        out_specs=[pl.BlockSpec((C, HEADS * RET_DV), lambda c, d: (c, 0)),
                   pl.BlockSpec((HEADS, RET_DK, RET_DV), lambda c, d: (0, 0, 0)),
                   pl.BlockSpec((None, RS_TB, HEADS * RET_DV), lambda c, d: (c, 0, 0)),
                   pl.BlockSpec((RS_TB, HEADS, RET_DK, RET_DV), lambda c, d: (c, 0, 0, 0))],
        scratch_shapes=[pltpu.VMEM((HEADS, RET_DK, RET_DV), F32),
                        pltpu.VMEM((HEADS, C, C), F32),
                        pltpu.VMEM((HEADS, C, C), F32),
                        pltpu.VMEM((HEADS, C, C), F32)],
    )
    ga_p, st_p, ga_s, st_s = pl.pallas_call(
        _retention_kernel,
        grid_spec=grid_spec,
        out_shape=[jax.ShapeDtypeStruct((SEQ, HEADS * RET_DV), BF16),
                   jax.ShapeDtypeStruct((HEADS, RET_DK, RET_DV), F32),
                   jax.ShapeDtypeStruct((n_steps, RS_TB, HEADS * RET_DV), F32),
                   jax.ShapeDtypeStruct((DEC_BATCH, HEADS, RET_DK, RET_DV), F32)],
        compiler_params=_params(1),
        name="retention",
    )(dec, z_p, z_p, z_p, z_p, cos_r, sin_r, cos_c, sin_c, z_s3, z_s3, z_s3, z_s3, cos_s, sin_s, state)
    return ga_p, st_p, ga_s.reshape(DEC_BATCH, HEADS * RET_DV), st_s


N_BLK = SEQ // MOBA_BLOCK
BIAS_W = 128
AUG_W = MOBA_DH + BIAS_W


FL_KC = 4
FL_CH = FL_KC * MOBA_BLOCK
N_CHUNK = SEQ // FL_CH
LOG2E = 1.4426950408889634
Q_SCALE = (MOBA_DH ** -0.5) * LOG2E


def _row_output_copies(i, k_ref, v_ref, kr_hbm, vr_hbm, sem):
    rows = pl.ds(pl.multiple_of(i * MOBA_BLOCK, MOBA_BLOCK), MOBA_BLOCK)
    copies = []
    for h in range(HEADS):
        cols = pl.ds(h * MOBA_DH, MOBA_DH)
        copies.append(pltpu.make_async_copy(k_ref.at[:, cols], kr_hbm.at[rows, h, :], sem.at[0]))
        copies.append(pltpu.make_async_copy(v_ref.at[:, cols], vr_hbm.at[rows, h, :], sem.at[1]))
    return copies


def _moba_prep_kernel(q_ref, k_ref, v_ref, qa_ref, ka_ref, vt_ref, kr_hbm, vr_hbm, km_scr, sem):
    i = pl.program_id(0)
    row_copies = _row_output_copies(i, k_ref, v_ref, kr_hbm, vr_hbm, sem)
    for cp in row_copies:
        cp.start()

    @pl.when(i == 0)
    def _():
        km_scr[...] = jnp.zeros_like(km_scr)

    blk = lax.broadcasted_iota(jnp.int32, (N_BLK, MOBA_BLOCK), 0)
    sub8 = lax.broadcasted_iota(jnp.int32, (8, MOBA_BLOCK), 0)
    lane = lax.broadcasted_iota(jnp.int32, (MOBA_BLOCK, BIAS_W), 1)
    onehot = jnp.where(lane == i, 1.0, 0.0).astype(BF16)
    pad = jnp.full((BIAS_W - N_BLK, MOBA_BLOCK), NEG_INF, F32)
    for h in range(HEADS):
        cols = slice(h * MOBA_DH, (h + 1) * MOBA_DH)
        q = q_ref[:, cols]
        k = k_ref[:, cols]
        s_t = lax.dot_general(km_scr[h], q, NT_DIMS, precision=lax.Precision.HIGHEST,
                              preferred_element_type=F32)
        groups = [s_t[r:r + 8, :] for r in range(0, N_BLK, 8)]
        ranks = [jnp.zeros((8, MOBA_BLOCK), F32) for _ in groups]
        for jp in range(N_BLK):
            sj = s_t[jp:jp + 1, :]
            is_past = jnp.where(jp < i, 1.0, 0.0)
            g0, r0 = divmod(jp, 8)
            for gi, grp in enumerate(groups):
                if gi < g0:
                    hit = jnp.where(sj > grp, is_past, 0.0)
                elif gi > g0:
                    hit = jnp.where(sj >= grp, is_past, 0.0)
                else:
                    hit = jnp.where(sub8 > r0, jnp.where(sj >= grp, is_past, 0.0),
                                    jnp.where(sj > grp, is_past, 0.0))
                ranks[gi] = ranks[gi] + hit
        rank = jnp.concatenate(ranks, axis=0)
        keep = ((blk < i) & (rank < MOBA_TOPK)) | (blk == i)
        bias_t = jnp.concatenate([jnp.where(keep, 0.0, NEG_INF), pad], axis=0)
        qa_ref[h, :, :MOBA_DH] = (q * Q_SCALE).astype(BF16)
        qa_ref[h, :, MOBA_DH:] = bias_t.T.astype(BF16)
        ka_ref[h, :, :MOBA_DH] = k.astype(BF16)
        ka_ref[h, :, MOBA_DH:] = onehot
        v = v_ref[:, cols]
        vt_ref[h] = v.T.astype(BF16)
        km_scr[h, pl.ds(i, 1), :] = jnp.mean(k, axis=0, keepdims=True)
    for cp in row_copies:
        cp.wait()


def _moba_prep(z_p):
    col = lambda base: (lambda i: (i, base // HD))
    out_idx = lambda i: (0, i, 0)
    return pl.pallas_call(
        _moba_prep_kernel,
        grid=(N_BLK,),
        in_specs=[pl.BlockSpec((MOBA_BLOCK, HD), col(COL_QB)),
                  pl.BlockSpec((MOBA_BLOCK, HD), col(COL_KB)),
                  pl.BlockSpec((MOBA_BLOCK, HD), col(COL_VB))],
        out_specs=[pl.BlockSpec((HEADS, MOBA_BLOCK, AUG_W), out_idx),
                   pl.BlockSpec((HEADS, MOBA_BLOCK, AUG_W), out_idx),
                   pl.BlockSpec((HEADS, None, MOBA_DH, MOBA_BLOCK), lambda i: (0, i // FL_KC, 0, i % FL_KC)),
                   pl.BlockSpec(memory_space=pl.ANY),
                   pl.BlockSpec(memory_space=pl.ANY)],
        out_shape=[jax.ShapeDtypeStruct((HEADS, SEQ, AUG_W), BF16),
                   jax.ShapeDtypeStruct((HEADS, SEQ, AUG_W), BF16),
                   jax.ShapeDtypeStruct((HEADS, N_CHUNK, MOBA_DH, FL_CH), BF16),
                   jax.ShapeDtypeStruct((SEQ, HEADS, MOBA_DH), F32),
                   jax.ShapeDtypeStruct((SEQ, HEADS, MOBA_DH), F32)],
        scratch_shapes=[pltpu.VMEM((HEADS, N_BLK, MOBA_DH), F32), pltpu.SemaphoreType.DMA((2,))],
        compiler_params=_params(1),
        name="moba_prep",
    )(z_p, z_p, z_p)


FL_HEADS = 2


def _moba_flash_kernel(q_ref, k_ref, vt_ref, g_ref, o_ref, sa_ref, sb_ref):
    i = pl.program_id(1)
    last = i // FL_KC
    heads = range(FL_HEADS)

    def logits(hh, c):
        rows = pl.ds(pl.multiple_of(c * FL_CH, FL_CH), FL_CH)
        return lax.dot_general(k_ref[hh, rows, :], q_ref[hh], NT_DIMS, preferred_element_type=F32)

    def update(carry, s_t, vt):
        m, l, acc = carry
        m_new = jnp.maximum(m, jnp.max(s_t, axis=0, keepdims=True))
        alpha = jnp.exp2(m - m_new)
        p = jnp.exp2(s_t - m_new)
        l = alpha * l + jnp.sum(p, axis=0, keepdims=True)
        acc = alpha * acc + jnp.dot(vt, p.astype(BF16), preferred_element_type=F32)
        return m_new, l, acc

    for hh in heads:
        sa_ref[hh] = logits(hh, 0)

    def own_chunk(n_blocks):
        n = n_blocks * MOBA_BLOCK

        def run():
            rows = pl.ds(pl.multiple_of(last * FL_CH, FL_CH), n)
            kpos = last * FL_CH + lax.broadcasted_iota(jnp.int32, (n, MOBA_BLOCK), 0)
            qpos = i * MOBA_BLOCK + lax.broadcasted_iota(jnp.int32, (n, MOBA_BLOCK), 1)
            out = []
            for hh in heads:
                s_t = lax.dot_general(k_ref[hh, rows, :], q_ref[hh], NT_DIMS, preferred_element_type=F32)
                s_t = jnp.where(kpos <= qpos, s_t, NEG_INF)
                m = jnp.max(s_t, axis=0, keepdims=True)
                p = jnp.exp2(s_t - m)
                l = jnp.sum(p, axis=0, keepdims=True)
                acc = jnp.dot(vt_ref[hh, last, :, 0:n], p.astype(BF16), preferred_element_type=F32)
                out.append((m, l, acc))
            return tuple(out)
        return run

    carries = lax.switch(i % FL_KC, [own_chunk(b + 1) for b in range(FL_KC)])

    def consume(carries, s_ref, c):
        return tuple(update(carries[hh], s_ref[hh], vt_ref[hh, c]) for hh in heads)

    def pair(t, carries):
        c = 2 * t
        for hh in heads:
            sb_ref[hh] = logits(hh, c + 1)
        carries = consume(carries, sa_ref, c)
        for hh in heads:
            sa_ref[hh] = logits(hh, c + 2)
        return consume(carries, sb_ref, c + 1)

    carries = lax.fori_loop(0, last // 2, pair, tuple(carries))
    carries = lax.cond(last % 2 == 1, lambda cs: consume(cs, sa_ref, last - 1), lambda cs: cs, carries)
    for hh in heads:
        m, l, acc = carries[hh]
        cols = slice(hh * MOBA_DH, (hh + 1) * MOBA_DH)
        g = g_ref[:, cols]
        o_ref[:, cols] = ((g * jax.nn.sigmoid(g)) * (acc / l).T).astype(o_ref.dtype)


PAST_BLOCKS = PAST_LEN // MOBA_BLOCK


PAGE_ROWS = PAGE_SIZE * HEADS
PAGES_PER_BLOCK = MOBA_BLOCK // PAGE_SIZE


def _sample_page_copies(pt_ref, seq, slot, ck_hbm, cv_hbm, kbuf, vbuf, sem):
    copies = []
    for p in range(N_PAGES):
        page = pt_ref[seq, p]
        copies.append(pltpu.make_async_copy(ck_hbm.at[page], kbuf.at[slot, p], sem.at[0, slot]))
        copies.append(pltpu.make_async_copy(cv_hbm.at[page], vbuf.at[slot, p], sem.at[1, slot]))
    return copies


def _moba_sample_step(step, n_steps, pt_ref, x_ref, ck_hbm, cv_hbm, o_ref, kbuf, vbuf, sem):
    slot = step % 2
    copies = functools.partial(_sample_page_copies, pt_ref, ck_hbm=ck_hbm, cv_hbm=cv_hbm,
                               kbuf=kbuf, vbuf=vbuf, sem=sem)

    @pl.when(step == 0)
    def _():
        for cp in copies(0, 0):
            cp.start()

    @pl.when(step + 1 < n_steps)
    def _():
        for cp in copies(step + 1, 1 - slot):
            cp.start()

    for cp in copies(step, slot):
        cp.wait()

    q = x_ref[0:HEADS, :]
    k_new = x_ref[HEADS:2 * HEADS, :]
    v_new = x_ref[2 * HEADS:3 * HEADS, :]
    g = x_ref[3 * HEADS:4 * HEADS, :]
    qs = q * (MOBA_DH ** -0.5)
    q16 = qs.astype(BF16)

    score = []
    for j in range(PAST_BLOCKS):
        ksum = sum(jnp.sum(kbuf[slot, PAGES_PER_BLOCK * j + r].reshape(PAGE_SIZE, HEADS, MOBA_DH), axis=0)
                   for r in range(PAGES_PER_BLOCK))
        score.append(jnp.sum(q * ksum, axis=-1, keepdims=True))
    keep = []
    for j in range(PAST_BLOCKS):
        rank = jnp.zeros((HEADS, 1), F32)
        for jp in range(PAST_BLOCKS):
            if jp != j:
                beats = (score[jp] >= score[j]) if jp < j else (score[jp] > score[j])
                rank = rank + jnp.where(beats, 1.0, 0.0)
        keep.append(jnp.where(rank < MOBA_TOPK, 1.0, 0.0))
    packed = jnp.zeros((HEADS, 1), F32)
    before = jnp.zeros((HEADS, 1), F32)
    for j in range(PAST_BLOCKS):
        for t in range(MOBA_TOPK):
            packed = packed + jnp.where(before == t, keep[j] * float(j * PAST_BLOCKS ** t), 0.0)
        before = before + keep[j]
    packed = packed.astype(jnp.int32)

    def head_rows(buf, h, t):
        blk = (packed[h, 0] // (PAST_BLOCKS ** t)) % PAST_BLOCKS
        return jnp.concatenate(
            [buf[slot, PAGES_PER_BLOCK * blk + r, pl.ds(h, PAGE_SIZE, stride=HEADS), :]
             for r in range(PAGES_PER_BLOCK)], axis=0).astype(BF16)

    logits = []
    for t in range(MOBA_TOPK):
        rows = []
        for h in range(HEADS):
            s = lax.dot_general(q16, head_rows(kbuf, h, t), NT_DIMS, preferred_element_type=F32)
            rows.append(s[h:h + 1, :])
        logits.append(jnp.concatenate(rows, axis=0))
    s_self = jnp.sum(qs * k_new, axis=-1, keepdims=True)
    m = functools.reduce(jnp.maximum, [jnp.max(s, axis=-1, keepdims=True) for s in logits] + [s_self])
    probs = [jnp.exp(s - m) for s in logits]
    p_self = jnp.exp(s_self - m)
    l = functools.reduce(jnp.add, [jnp.sum(p, axis=-1, keepdims=True) for p in probs] + [p_self])
    probs16 = [p.astype(BF16) for p in probs]
    y_rows = []
    for h in range(HEADS):
        acc = jnp.zeros((1, MOBA_DH), F32)
        for t in range(MOBA_TOPK):
            r = jnp.dot(probs16[t], head_rows(vbuf, h, t), preferred_element_type=F32)
            acc = acc + r[h:h + 1, :]
        y_rows.append(acc)
    y = (jnp.concatenate(y_rows, axis=0) + p_self * v_new) / l
    o_ref[...] = (g * jax.nn.sigmoid(g)) * y


FL_STEPS_PER_GROUP = N_BLK
assert (HEADS // FL_HEADS) * FL_STEPS_PER_GROUP == DEC_BATCH


def _moba_fused_kernel(pt_ref, q_ref, k_ref, vt_ref, g_ref, x_ref, ck_hbm, cv_hbm, o_ref, os_ref,
                       sa_ref, sb_ref, kbuf, vbuf, sem):
    step = pl.program_id(0) * FL_STEPS_PER_GROUP + pl.program_id(1)
    _moba_sample_step(step, DEC_BATCH, pt_ref, x_ref, ck_hbm, cv_hbm, os_ref, kbuf, vbuf, sem)
    _moba_flash_kernel(q_ref, k_ref, vt_ref, g_ref, o_ref, sa_ref, sb_ref)


def _moba_fused(page_table, q_aug, k_aug, v_t, z_p, x_s, cache_k, cache_v):
    gw = FL_HEADS * MOBA_DH
    seq = lambda h, i: h * FL_STEPS_PER_GROUP + i
    resident = pl.Buffered(1)
    grid_spec = pltpu.PrefetchScalarGridSpec(
        num_scalar_prefetch=1,
        grid=(HEADS // FL_HEADS, N_BLK),
        in_specs=([pl.BlockSpec((FL_HEADS, MOBA_BLOCK, AUG_W), lambda h, i, pt: (h, i, 0)),
                   pl.BlockSpec((FL_HEADS, SEQ, AUG_W), lambda h, i, pt: (h, 0, 0), pipeline_mode=resident),
                   pl.BlockSpec((FL_HEADS, N_CHUNK, MOBA_DH, FL_CH), lambda h, i, pt: (h, 0, 0, 0),
                                pipeline_mode=resident),
                   pl.BlockSpec((MOBA_BLOCK, gw), lambda h, i, pt: (i, COL_GB // gw + h)),
                   pl.BlockSpec((None, 4 * HEADS, MOBA_DH), lambda h, i, pt: (seq(h, i), 0, 0)),
                   pl.BlockSpec(memory_space=pl.ANY),
                   pl.BlockSpec(memory_space=pl.ANY)]),
        out_specs=[pl.BlockSpec((MOBA_BLOCK, gw), lambda h, i, pt: (i, h)),
                   pl.BlockSpec((None, HEADS, MOBA_DH), lambda h, i, pt: (seq(h, i), 0, 0))],
        scratch_shapes=[pltpu.VMEM((FL_HEADS, FL_CH, MOBA_BLOCK), F32),
                        pltpu.VMEM((FL_HEADS, FL_CH, MOBA_BLOCK), F32),
                        pltpu.VMEM((2, N_PAGES, PAGE_ROWS, MOBA_DH), F32),
                        pltpu.VMEM((2, N_PAGES, PAGE_ROWS, MOBA_DH), F32),
                        pltpu.SemaphoreType.DMA((2, 2))],
    )
    return pl.pallas_call(
        _moba_fused_kernel,
        grid_spec=grid_spec,
        out_shape=[jax.ShapeDtypeStruct((SEQ, HEADS * MOBA_DH), BF16),
                   jax.ShapeDtypeStruct((DEC_BATCH, HEADS, MOBA_DH), F32)],
        compiler_params=_params(2),
        name="moba_fused",
    )(page_table, q_aug, k_aug, v_t, z_p, x_s, cache_k, cache_v)


def _out_proj_kernel(ga_ref, gb_ref, zma_ref, zmb_ref, x_ref, bm_ref, wa_ref, wb_ref, wo_ref, gp_ref, y_ref):
    u_a = jnp.dot(ga_ref[...], wa_ref[...], preferred_element_type=F32)
    u_b = jnp.dot(gb_ref[...], wb_ref[...], preferred_element_type=F32)
    gate_a = jax.nn.sigmoid(zma_ref[...] + bm_ref[0:1, :])
    gate_b = jax.nn.sigmoid(zmb_ref[...] + bm_ref[1:2, :])
    merged = (gate_a * u_a + gate_b * u_b).astype(BF16)
    o = jnp.dot(merged, wo_ref[...], preferred_element_type=F32)
    on = o * lax.rsqrt(jnp.mean(o * o, axis=-1, keepdims=True) + NORM_EPS)
    y_ref[...] = x_ref[...] + on * gp_ref[...]


def _out_proj(ga, gb, z, x, b_merge, w_a, w_b, w_o, g_post, tile):
    rows = x.shape[0]
    resident = lambda shape: pl.BlockSpec(shape, lambda i: (0, 0), pipeline_mode=pl.Buffered(1))
    return pl.pallas_call(
        _out_proj_kernel,
        grid=(rows // tile,),
        in_specs=[pl.BlockSpec((tile, HEADS * RET_DV), lambda i: (i, 0)),
                  pl.BlockSpec((tile, HD), lambda i: (i, 0)),
                  pl.BlockSpec((tile, D_MODEL), lambda i: (i, COL_ZM // D_MODEL)),
                  pl.BlockSpec((tile, D_MODEL), lambda i: (i, COL_ZM // D_MODEL + 1)),
                  pl.BlockSpec((tile, D_MODEL), lambda i: (i, 0)),
                  resident((2, D_MODEL)),
                  resident((HEADS * RET_DV, D_MODEL)),
                  resident((HD, D_MODEL)),
                  resident((D_MODEL, D_MODEL)),
                  resident((1, D_MODEL))],
        out_specs=pl.BlockSpec((tile, D_MODEL), lambda i: (i, 0)),
        out_shape=jax.ShapeDtypeStruct((rows, D_MODEL), F32),
        compiler_params=_params(1),
        name="out_proj",
    )(ga, gb, z, z, x, b_merge, w_a, w_b, w_o, g_post)


def _rope_tables(pos):
    half = RET_DK // 2
    inv = 1.0 / (RET_ROPE_BASE ** jnp.linspace(0.0, 1.0, half, dtype=F32))
    ang = pos.astype(F32)[:, None] * inv[None, :]
    cos, sin = jnp.cos(ang), jnp.sin(ang)
    return jnp.concatenate([cos, cos], axis=-1), jnp.concatenate([-sin, sin], axis=-1)


def kernel(x_prompt, x_sample, state_ret, cache_k, cache_v, page_table, norm_pre, w_in,
           w_a_out, w_b_out, b_merge, w_out, norm_post):
    xp = x_prompt.reshape(SEQ, D_MODEL)
    xs = x_sample.reshape(DEC_BATCH, D_MODEL)
    g_pre = norm_pre.reshape(1, D_MODEL)
    g_post = norm_post.reshape(1, D_MODEL)

    log_g = jnp.log(1.0 - 2.0 ** (-5.0 - jnp.arange(HEADS, dtype=F32)))
    dec = jnp.stack([log_g, jnp.exp(log_g * RET_CHUNK), jnp.exp(log_g * 1.0)])
    rope_row = _rope_tables(jnp.arange(RET_CHUNK, dtype=jnp.int32))
    rope_chunk = tuple(t.reshape(SEQ // RET_CHUNK, 1, RET_DK)
                       for t in _rope_tables(RET_CHUNK * jnp.arange(SEQ // RET_CHUNK, dtype=jnp.int32)))
    cos_s, sin_s = _rope_tables(PAST_LEN + jnp.arange(1, dtype=jnp.int32))

    hn_p = _rmsnorm(xp, g_pre, 1024)
    hn_s = _rmsnorm(xs, g_pre, DEC_BATCH)
    z_p, z_s = _in_proj(hn_p, hn_s, w_in.reshape(D_MODEL, PROJ_W))

    ga_p, st_p, ga_s, st_s = _retention(dec, z_p, rope_row, rope_chunk, z_s, (cos_s, sin_s),
                                        state_ret.reshape(DEC_BATCH, HEADS, RET_DK, RET_DV))

    q_aug, k_aug, v_t, k_rows_p, v_rows_p = _moba_prep(z_p)
    n_phys = cache_k.shape[1]
    gb_p, gb_s = _moba_fused(page_table, q_aug, k_aug, v_t, z_p,
                             z_s[:, COL_QB:COL_ZM].reshape(DEC_BATCH, 4 * HEADS, MOBA_DH),
                             cache_k.reshape(n_phys, PAGE_ROWS, MOBA_DH),
                             cache_v.reshape(n_phys, PAGE_ROWS, MOBA_DH))

    w_a = w_a_out.reshape(HEADS * RET_DV, D_MODEL).astype(BF16)
    w_b = w_b_out.reshape(HD, D_MODEL).astype(BF16)
    w_o = w_out.reshape(D_MODEL, D_MODEL).astype(BF16)
    bm = b_merge.reshape(2, D_MODEL)
    y_p = _out_proj(ga_p, gb_p, z_p, xp, bm, w_a, w_b, w_o, g_post, 256)
    y_s = _out_proj(ga_s.astype(BF16), gb_s.reshape(DEC_BATCH, HD).astype(BF16), z_s, xs, bm,
                    w_a, w_b, w_o, g_post, DEC_BATCH)

    return (y_p.reshape(1, SEQ, D_MODEL),
            y_s.reshape(DEC_BATCH, 1, D_MODEL),
            st_p.reshape(1, 1, HEADS, RET_DK, RET_DV),
            st_s.reshape(1, DEC_BATCH, HEADS, RET_DK, RET_DV),
            k_rows_p.reshape(1, 1, SEQ, HEADS, MOBA_DH),
            v_rows_p.reshape(1, 1, SEQ, HEADS, MOBA_DH),
            z_s[:, COL_KB:COL_VB].reshape(1, DEC_BATCH, 1, HEADS, MOBA_DH),
            z_s[:, COL_VB:COL_GB].reshape(1, DEC_BATCH, 1, HEADS, MOBA_DH))
```
